```python
import math
import jax, jax.numpy as jnp
from jax import lax
import numpy as np

D_MODEL = 1024
BATCH = 16
SEQ = 4096
DEPTH = 4

PLE_DIM = 256
ROPE_THETA = 500000.0
NORM_EPS = 1e-6
Q_BLOCK = 128

DA_HEADS = 4
DA_DQK = 64
DA_DV = 2 * DA_DQK
DA_ROT = DA_DQK // 4

SSD_HEADS = 8
SSD_HEADDIM = 64
SSD_DINNER = SSD_HEADS * SSD_HEADDIM
SSD_GROUPS = 2
SSD_DSTATE = 128
SSD_CONV = 4
SSD_CHUNK = 128
SSD_CONV_DIM = SSD_DINNER + 2 * SSD_GROUPS * SSD_DSTATE

GLA_HEADS = 4
GLA_DK = 64
GLA_DV = 128
GLA_GATE_RANK = 16
GLA_GATE_NORMALIZER = 16.0
GLA_CHUNK = 64

DA_WIDTH = DA_HEADS * DA_DV
GLA_WIDTH = GLA_HEADS * GLA_DV
MIX_WIDTH = DA_WIDTH + SSD_DINNER + GLA_WIDTH
D_FF = -(-8 * D_MODEL // (3 * 256)) * 256

IN_SIZES = (
    DA_HEADS * 2 * DA_DQK,
    DA_HEADS * 2 * DA_DQK,
    DA_WIDTH,
    SSD_DINNER,
    SSD_CONV_DIM,
    SSD_HEADS,
    GLA_HEADS * GLA_DK,
    GLA_HEADS * GLA_DK,
    GLA_WIDTH,
    GLA_WIDTH,
    GLA_GATE_RANK,
)
IN_COLS = sum(IN_SIZES)
IN_SPLITS = tuple(int(v) for v in np.cumsum(IN_SIZES)[:-1])

kernel_name = 'hymba_style_diffattn_ssd_gla_trunk'


def rms_norm(x, gain):
    xf = x.astype(jnp.float32)
    y = xf * lax.rsqrt(jnp.mean(xf * xf, axis=-1, keepdims=True) + NORM_EPS)
    return (y * gain.astype(jnp.float32)).astype(x.dtype)


def rope_tables(positions):
    inv_freq = ROPE_THETA ** (-jnp.arange(0, DA_ROT, 2, dtype=jnp.float32) / DA_ROT)
    ang = positions.astype(jnp.float32)[..., None] * inv_freq
    return jnp.cos(ang), jnp.sin(ang)


def partial_rope(x, cos, sin):
    half = DA_ROT // 2
    x1 = x[..., :half].astype(jnp.float32)
    x2 = x[..., half:DA_ROT].astype(jnp.float32)
    c = cos[:, :, None, :]
    s = sin[:, :, None, :]
    r1 = (x1 * c - x2 * s).astype(x.dtype)
    r2 = (x2 * c + x1 * s).astype(x.dtype)
    return jnp.concatenate([r1, r2, x[..., DA_ROT:]], axis=-1)


def diff_attention(q, k, v, positions, g_q, g_k, lq1, lk1, lq2, lk2, g_sub, lambda_init):
    B, S = q.shape[:2]
    q = rms_norm(q.reshape(B, S, DA_HEADS * 2, DA_DQK), g_q)
    k = rms_norm(k.reshape(B, S, DA_HEADS * 2, DA_DQK), g_k)
    v = v.reshape(B, S, DA_HEADS, DA_DV)
    cos, sin = rope_tables(positions)
    q = partial_rope(q, cos, sin).reshape(B, S, DA_HEADS, 2, DA_DQK)
    k = partial_rope(k, cos, sin).reshape(B, S, DA_HEADS, 2, DA_DQK)
    lam = (jnp.exp(jnp.sum(lq1.astype(jnp.float32) * lk1.astype(jnp.float32)))
           - jnp.exp(jnp.sum(lq2.astype(jnp.float32) * lk2.astype(jnp.float32)))
           + lambda_init)
    scale = DA_DQK ** -0.5
    n_blk = S // Q_BLOCK
    qb = q.reshape(B, n_blk, Q_BLOCK, DA_HEADS, 2, DA_DQK).transpose(1, 0, 2, 3, 4, 5)
    key_idx = jnp.arange(S)

    def one_block(args):
        qi, bi = args
        s = jnp.einsum('bqhcd,bkhcd->bhcqk', qi, k,
                       preferred_element_type=jnp.float32) * scale
        q_idx = bi * Q_BLOCK + jnp.arange(Q_BLOCK)
        causal = key_idx[None, :] <= q_idx[:, None]
        s = jnp.where(causal, s, -jnp.inf)
        pr = jax.nn.softmax(s, axis=-1)
        attn = pr[:, :, 0] - lam * pr[:, :, 1]
        return jnp.einsum('bhqk,bkhd->bqhd', attn.astype(v.dtype), v)

    out = lax.map(one_block, (qb, jnp.arange(n_blk)))
    out = out.transpose(1, 0, 2, 3, 4).reshape(B, S, DA_HEADS, DA_DV)
    out = rms_norm(out, g_sub) * (1.0 - lambda_init)
    return out.reshape(B, S, DA_WIDTH)


def ssd_mixer(z, xbc, dt_raw, conv_w, conv_b, dt_bias, a_log, d_skip, g_norm):
    B, S = z.shape[:2]
    G, R, P, N, L = SSD_GROUPS, SSD_HEADS // SSD_GROUPS, SSD_HEADDIM, SSD_DSTATE, SSD_CHUNK
    nc = S // L
    xbc = lax.conv_general_dilated(
        xbc, conv_w[:, None, :], window_strides=(1,), padding=[(SSD_CONV - 1, 0)],
        dimension_numbers=('NWC', 'WIO', 'NWC'), feature_group_count=SSD_CONV_DIM) + conv_b
    xbc = jax.nn.silu(xbc)
    xs, bs, cs = jnp.split(xbc, [SSD_DINNER, SSD_DINNER + G * N], axis=-1)
    dt = jax.nn.softplus(dt_raw.astype(jnp.float32) + dt_bias.astype(jnp.float32))
    a = -jnp.exp(a_log.astype(jnp.float32))
    xh = xs.astype(jnp.float32).reshape(B, nc, L, G, R, P)
    xdt = xh * dt.reshape(B, nc, L, G, R)[..., None]
    bm = bs.astype(jnp.float32).reshape(B, nc, L, G, N)
    cm = cs.astype(jnp.float32).reshape(B, nc, L, G, N)
    cum = jnp.cumsum((dt * a).reshape(B, nc, L, G, R), axis=2)
    seg = cum[:, :, :, None] - cum[:, :, None, :]
    causal = jnp.tril(jnp.ones((L, L), dtype=bool))[:, :, None, None]
    decay = jnp.exp(jnp.where(causal, seg, -jnp.inf))
    cb = jnp.einsum('bclgn,bcsgn->bclsg', cm, bm)
    y_diag = jnp.einsum('bclsgr,bcsgrp->bclgrp', cb[..., None] * decay, xdt)
    decay_to_end = jnp.exp(cum[:, :, -1:] - cum)
    states = jnp.einsum('bclgn,bclgrp->bcgrpn', bm, xdt * decay_to_end[..., None])
    chunk_decay = jnp.exp(cum[:, :, -1])

    def step(h, inp):
        st, dec = inp
        return dec[..., None, None] * h + st, h

    h0 = jnp.zeros((B, G, R, P, N), jnp.float32)
    _, prev = lax.scan(step, h0, (states.transpose(1, 0, 2, 3, 4, 5),
                                  chunk_decay.transpose(1, 0, 2, 3)))
    prev = prev.transpose(1, 0, 2, 3, 4, 5)
    y_off = jnp.einsum('bclgn,bcgrpn->bclgrp', cm, prev) * jnp.exp(cum)[..., None]
    y = y_diag + y_off + d_skip.astype(jnp.float32).reshape(G, R)[:, :, None] * xh
    y = y.reshape(B, S, SSD_DINNER) * jax.nn.silu(z.astype(jnp.float32))
    y = rms_norm(y.reshape(B, S, G, SSD_DINNER // G), g_norm.reshape(G, SSD_DINNER // G))
    return y.reshape(B, S, SSD_DINNER).astype(z.dtype)


def gla_mixer(q, k, v, g_out, gate_lr, w_gate2, b_gate, g_norm):
    B, S = q.shape[:2]
    H, DK, DV, L = GLA_HEADS, GLA_DK, GLA_DV, GLA_CHUNK
    nc = S // L
    gk = jax.nn.log_sigmoid((gate_lr @ w_gate2 + b_gate).astype(jnp.float32)) / GLA_GATE_NORMALIZER
    qf = q.astype(jnp.float32).reshape(B, nc, L, H, DK) * (DK ** -0.5)
    kf = k.astype(jnp.float32).reshape(B, nc, L, H, DK)
    vf = v.astype(jnp.float32).reshape(B, nc, L, H, DV)
    bcum = jnp.cumsum(gk.reshape(B, nc, L, H, DK), axis=2)
    q_dec = qf * jnp.exp(bcum)
    k_inv = kf * jnp.exp(-bcum)
    att = jnp.einsum('bclhd,bcshd->bchls', q_dec, k_inv)
    att = jnp.where(jnp.tril(jnp.ones((L, L), dtype=bool)), att, 0.0)
    o_intra = jnp.einsum('bchls,bcshv->bclhv', att, vf)
    k_end = kf * jnp.exp(bcum[:, :, -1:] - bcum)
    states = jnp.einsum('bclhd,bclhv->bchdv', k_end, vf)
    chunk_decay = jnp.exp(bcum[:, :, -1])

    def step(st_prev, inp):
        st, dec = inp
        return dec[..., None] * st_prev + st, st_prev

    s0 = jnp.zeros((B, H, DK, DV), jnp.float32)
    _, prev = lax.scan(step, s0, (states.transpose(1, 0, 2, 3, 4),
                                  chunk_decay.transpose(1, 0, 2, 3)))
    prev = prev.transpose(1, 0, 2, 3, 4)
    o_inter = jnp.einsum('bclhd,bchdv->bclhv', q_dec, prev)
    o = (o_intra + o_inter).reshape(B, S, H, DV)
    o = rms_norm(o, g_norm) * jax.nn.silu(g_out.astype(jnp.float32).reshape(B, S, H, DV))
    return o.reshape(B, S, GLA_WIDTH).astype(q.dtype)


def setup_inputs(seed: int = 0) -> dict:
    key = jax.random.key(seed)
    ks = iter(jax.random.split(key, 48))

    def nrm(shape, scale):
        return jax.random.normal(next(ks), shape, jnp.float32) * scale

    def gain(shape):
        return 1.0 + nrm(shape, 0.02)

    x = nrm((BATCH, SEQ, D_MODEL), 1.0)
    p = nrm((DEPTH, BATCH, SEQ, PLE_DIM), 1.0)
    positions = jnp.broadcast_to(jnp.arange(SEQ, dtype=jnp.int32)[None, :], (BATCH, SEQ))
    dt0 = jnp.exp(jax.random.uniform(next(ks), (DEPTH, SSD_HEADS), jnp.float32)
                  * (math.log(0.1) - math.log(0.001)) + math.log(0.001))
    ssd_dt_bias = dt0 + jnp.log(-jnp.expm1(-dt0))
    ssd_a_log = jnp.log(jax.random.uniform(next(ks), (DEPTH, SSD_HEADS), jnp.float32,
                                           minval=1.0, maxval=16.0))
    return {
        'x': x,
        'p': p,
        'positions': positions,
        'attn_norm': gain((DEPTH, D_MODEL)),
        'w_in': nrm((DEPTH, D_MODEL, IN_COLS), D_MODEL ** -0.5),
        'da_q_norm': gain((DEPTH, DA_DQK)),
        'da_k_norm': gain((DEPTH, DA_DQK)),
        'da_lambda_q1': nrm((DEPTH, DA_DQK), 0.1),
        'da_lambda_k1': nrm((DEPTH, DA_DQK), 0.1),
        'da_lambda_q2': nrm((DEPTH, DA_DQK), 0.1),
        'da_lambda_k2': nrm((DEPTH, DA_DQK), 0.1),
        'da_sub_norm': gain((DEPTH, DA_DV)),
        'ssd_conv_w': nrm((DEPTH, SSD_CONV, SSD_CONV_DIM), SSD_CONV ** -0.5),
        'ssd_conv_b': nrm((DEPTH, SSD_CONV_DIM), 0.02),
        'ssd_dt_bias': ssd_dt_bias,
        'ssd_a_log': ssd_a_log,
        'ssd_d': 1.0 + nrm((DEPTH, SSD_HEADS), 0.1),
        'ssd_norm': gain((DEPTH, SSD_DINNER)),
        'gla_w_gate2': nrm((DEPTH, GLA_GATE_RANK, GLA_HEADS * GLA_DK), GLA_GATE_RANK ** -0.5),
        'gla_b_gate': nrm((DEPTH, GLA_HEADS * GLA_DK), 0.02),
        'gla_norm': gain((DEPTH, GLA_DV)),
        'w_out': nrm((DEPTH, MIX_WIDTH, D_MODEL), MIX_WIDTH ** -0.5),
        'ffn_norm': gain((DEPTH, D_MODEL)),
        'w_ffn_gate': nrm((DEPTH, D_MODEL, D_FF), D_MODEL ** -0.5),
        'w_ffn_up': nrm((DEPTH, D_MODEL, D_FF), D_MODEL ** -0.5),
        'w_ffn_down': nrm((DEPTH, D_FF, D_MODEL), D_FF ** -0.5),
        'ple_w_proj': nrm((DEPTH, PLE_DIM, D_MODEL), PLE_DIM ** -0.5),
        'ple_w_gate': nrm((DEPTH, D_MODEL, D_MODEL), D_MODEL ** -0.5),
    }


def reference(x, p, positions, attn_norm, w_in, da_q_norm, da_k_norm, da_lambda_q1,
              da_lambda_k1, da_lambda_q2, da_lambda_k2, da_sub_norm, ssd_conv_w, ssd_conv_b,
              ssd_dt_bias, ssd_a_log, ssd_d, ssd_norm, gla_w_gate2, gla_b_gate, gla_norm,
              w_out, ffn_norm, w_ffn_gate, w_ffn_up, w_ffn_down, ple_w_proj, ple_w_gate):
    h = x
    for i in range(DEPTH):
        lambda_init = 0.8 - 0.6 * math.exp(-0.3 * i)
        a = rms_norm(h, attn_norm[i])
        proj = a @ w_in[i]
        (da_q, da_k, da_v, ssd_z, ssd_xbc, ssd_dt,
         gla_q, gla_k, gla_v, gla_g, gla_lr) = jnp.split(proj, IN_SPLITS, axis=-1)
        y_da = diff_attention(da_q, da_k, da_v, positions, da_q_norm[i], da_k_norm[i],
                              da_lambda_q1[i], da_lambda_k1[i], da_lambda_q2[i],
                              da_lambda_k2[i], da_sub_norm[i], lambda_init)
        y_ssd = ssd_mixer(ssd_z, ssd_xbc, ssd_dt, ssd_conv_w[i], ssd_conv_b[i],
                          ssd_dt_bias[i], ssd_a_log[i], ssd_d[i], ssd_norm[i])
        y_gla = gla_mixer(gla_q, gla_k, gla_v, gla_g, gla_lr, gla_w_gate2[i],
                          gla_b_gate[i], gla_norm[i])
        mix = jnp.concatenate([y_da, y_ssd.astype(y_da.dtype), y_gla.astype(y_da.dtype)], axis=-1)
        h = h + mix @ w_out[i]
        f = rms_norm(h, ffn_norm[i])
        h = h + (jax.nn.silu(f @ w_ffn_gate[i]) * (f @ w_ffn_up[i])) @ w_ffn_down[i]
        h = h + jax.nn.sigmoid(h @ ple_w_gate[i]) * (p[i] @ ple_w_proj[i])
    return h
```

```python
import functools
import math

import jax
import jax.numpy as jnp
import numpy as np
from jax import lax
from jax.experimental import pallas as pl
from jax.experimental.pallas import tpu as pltpu

D_MODEL = 1024
PLE_DIM = 256
ROPE_THETA = 500000.0
NORM_EPS = 1e-6

DA_HEADS = 4
DA_DQK = 64
DA_DV = 2 * DA_DQK
DA_ROT = DA_DQK // 4
DA_WIDTH = DA_HEADS * DA_DV

SSD_HEADS = 8
SSD_HEADDIM = 64
SSD_DINNER = SSD_HEADS * SSD_HEADDIM
SSD_GROUPS = 2
SSD_DSTATE = 128
SSD_CONV = 4
SSD_CHUNK = 128
SSD_CONV_DIM = SSD_DINNER + 2 * SSD_GROUPS * SSD_DSTATE

GLA_HEADS = 4
GLA_DK = 64
GLA_DV = 128
GLA_GATE_RANK = 16
GLA_GATE_NORMALIZER = 16.0
GLA_CHUNK = 64
GLA_WIDTH = GLA_HEADS * GLA_DV

MIX_WIDTH = DA_WIDTH + SSD_DINNER + GLA_WIDTH
D_FF = -(-8 * D_MODEL // (3 * 256)) * 256

LANES = 128
SUBLANES = 8
VMEM_LIMIT = 56 * 1024 * 1024

DT_LANE0 = 0
LR_LANE0 = SSD_HEADS

TM_PROJ = 512
TM_POST = 512
TQ_ATTN = 512
TK_ATTN = 512
TS_SSD = 512
TS_GLA = 512
ROPE_BLK = 2048

NEG_BIG = -1e30
F32 = jnp.float32
BF16 = jnp.bfloat16


def _dot(a, b):
    return jnp.dot(a, b, preferred_element_type=F32)


def _dot_nt(a, b):
    return lax.dot_general(a, b, (((1,), (1,)), ((), ())), preferred_element_type=F32)


def _dot_tn(a, b):
    return lax.dot_general(a, b, (((0,), (0,)), ((), ())), preferred_element_type=F32)


def _split3(x):
    hi = x.astype(BF16)
    r1 = x - hi.astype(F32)
    mid = r1.astype(BF16)
    lo = (r1 - mid.astype(F32)).astype(BF16)
    return hi, mid, lo


def _dot_exact_lhs(m_bf16, x):
    hi, mid, lo = _split3(x)
    return _dot(m_bf16, hi) + _dot(m_bf16, mid) + _dot(m_bf16, lo)


def _dot_exact_rhs(x, e_bf16):
    hi, mid, lo = _split3(x)
    return _dot(hi, e_bf16) + _dot(mid, e_bf16) + _dot(lo, e_bf16)


def _sigmoid(x):
    return 1.0 / (1.0 + jnp.exp(-x))


def _silu(x):
    return x * _sigmoid(x)


def _softplus(x):
    return jnp.maximum(x, 0.0) + jnp.log1p(jnp.exp(-jnp.abs(x)))


def _cparams(n_axes):
    return pltpu.CompilerParams(dimension_semantics=("arbitrary",) * n_axes,
                                vmem_limit_bytes=VMEM_LIMIT)


def _const_spec(shape):
    nd = len(shape)
    return pl.BlockSpec(shape, lambda *_: (0,) * nd, pipeline_mode=pl.Buffered(1))


def _rope_kernel(pos_ref, invf_ref, cos_ref, sin_ref):
    ang = pos_ref[...].astype(F32) * invf_ref[...]
    cos_ref[...] = jnp.cos(ang)
    sin_ref[...] = jnp.sin(ang)


def _rope_tables(positions):
    t = positions.size
    half = DA_ROT // 2
    blk = min(ROPE_BLK, t)
    inv_freq = (ROPE_THETA ** (-jnp.arange(0, DA_ROT, 2, dtype=F32) / DA_ROT)).reshape(half, 1)
    cos8, sin8 = pl.pallas_call(
        _rope_kernel,
        out_shape=(jax.ShapeDtypeStruct((half, t), F32),) * 2,
        grid=(t // blk,),
        in_specs=[pl.BlockSpec((1, blk), lambda i: (0, i)),
                  pl.BlockSpec((half, 1), lambda i: (0, 0))],
        out_specs=(pl.BlockSpec((half, blk), lambda i: (0, i)),) * 2,
        compiler_params=_cparams(1),
        name="rope_tables",
    )(positions.reshape(1, t), inv_freq)
    c = cos8.T
    s = sin8.T
    rest = DA_DQK - DA_ROT
    c64 = jnp.concatenate([c, c, jnp.ones((t, rest), F32)], axis=1)
    s64 = jnp.concatenate([-s, s, jnp.zeros((t, rest), F32)], axis=1)
    reps = LANES // DA_DQK
    return jnp.tile(c64, (1, reps)), jnp.tile(s64, (1, reps))


def _in_proj_kernel(h_ref, gain_ref, w1_ref, w2_ref, w3_ref, gmat_ref, qg_ref, kg_ref, c_ref, s_ref,
                    q_out, k_out, v_out, z_out, xbc_out, small_out, gq_out, gk_out, gv_out, gg_out):
    h = h_ref[...]
    ms = jnp.mean(h * h, axis=-1, keepdims=True)
    a = (h * lax.rsqrt(ms + NORM_EPS) * gain_ref[...]).astype(BF16)
    tm = h.shape[0]
    qk_w = DA_HEADS * 2 * DA_DQK
    n_slab = qk_w // LANES
    half = DA_ROT // 2

    c_tab = c_ref[...]
    s_tab = s_ref[...]
    lane = lax.broadcasted_iota(jnp.int32, (tm, LANES), 1) & (DA_DQK - 1)
    first_half = lane < half

    def norm_rope(raw, g_ref, scale):
        msq = _dot((raw * raw).astype(BF16), gmat_ref[...])
        n = raw * lax.rsqrt(msq + NORM_EPS) * g_ref[...]
        outs = []
        for i in range(n_slab):
            x = n[:, i * LANES:(i + 1) * LANES]
            partner = jnp.where(first_half, pltpu.roll(x, LANES - half, 1), pltpu.roll(x, half, 1))
            outs.append((x * c_tab + partner * s_tab) * scale)
        return jnp.concatenate(outs, axis=1).astype(BF16)

    q_out[...] = norm_rope(_dot(a, w1_ref[:, 0:qk_w]), qg_ref, DA_DQK ** -0.5)
    k_out[...] = norm_rope(_dot(a, w1_ref[:, qk_w:2 * qk_w]), kg_ref, 1.0)
    o = 2 * qk_w
    v_out[...] = _dot(a, w1_ref[:, o:o + DA_WIDTH]).astype(BF16)
    o += DA_WIDTH
    z_out[...] = _dot(a, w1_ref[:, o:o + SSD_DINNER])
    o += SSD_DINNER
    xbc_out[...] = _dot(a, w1_ref[:, o:o + SSD_CONV_DIM])
    small_out[...] = _dot(a, w3_ref[...])
    gqk = GLA_HEADS * GLA_DK
    gq_out[...] = _dot(a, w2_ref[:, 0:gqk])
    gk_out[...] = _dot(a, w2_ref[:, gqk:2 * gqk])
    gv_out[...] = _dot(a, w2_ref[:, 2 * gqk:2 * gqk + GLA_WIDTH]).astype(BF16)
    gg_out[...] = _dot(a, w2_ref[:, 2 * gqk + GLA_WIDTH:2 * gqk + 2 * GLA_WIDTH])


def _in_proj(h, gain, w1, w2, w3, gmat, qg, kg, c_tab, s_tab):
    t = h.shape[0]
    tm = min(TM_PROJ, t)
    qk_w = DA_HEADS * 2 * DA_DQK
    gqk = GLA_HEADS * GLA_DK
    widths = [(qk_w, BF16), (qk_w, BF16), (DA_WIDTH, BF16), (SSD_DINNER, F32), (SSD_CONV_DIM, F32),
              (LANES, F32), (gqk, F32), (gqk, F32), (GLA_WIDTH, BF16), (GLA_WIDTH, F32)]
    row = lambda w: pl.BlockSpec((tm, w), lambda i: (i, 0))
    return pl.pallas_call(
        _in_proj_kernel,
        out_shape=tuple(jax.ShapeDtypeStruct((t, w), d) for w, d in widths),
        grid=(t // tm,),
        in_specs=[row(D_MODEL), _const_spec(gain.shape), _const_spec(w1.shape), _const_spec(w2.shape),
                  _const_spec(w3.shape), _const_spec(gmat.shape), _const_spec(qg.shape),
                  _const_spec(kg.shape), row(LANES), row(LANES)],
        out_specs=tuple(row(w) for w, _ in widths),
        compiler_params=_cparams(1),
        name="in_proj",
    )(h, gain, w1, w2, w3, gmat, qg, kg, c_tab, s_tab)


def _attn_kernel(q_ref, k_ref, v_ref, lam_ref, gsub_ref, o_ref, acc1_ref, acc2_ref, *, tq, tk):
    qi = pl.program_id(2)
    q = q_ref[0]
    lane = lax.broadcasted_iota(jnp.int32, q.shape, 1)
    zero = jnp.zeros_like(q)
    q1 = jnp.where(lane < DA_DQK, q, zero)
    q2 = jnp.where(lane >= DA_DQK, q, zero)
    ones_col = (lax.broadcasted_iota(jnp.int32, (tk, LANES), 1) == 0).astype(BF16)
    acc1_ref[...] = jnp.zeros_like(acc1_ref)
    acc2_ref[...] = jnp.zeros_like(acc2_ref)

    def block(j, carry, diag):
        m1, m2 = carry
        start = pl.multiple_of(j * tk, tk)
        kj = k_ref[0, pl.ds(start, tk), :]
        vext = jnp.concatenate([v_ref[0, pl.ds(start, tk), :], ones_col], axis=1)
        new = []
        for qc, m, acc_ref in ((q1, m1, acc1_ref), (q2, m2, acc2_ref)):
            s = _dot_nt(qc, kj)
            if diag:
                row = lax.broadcasted_iota(jnp.int32, s.shape, 0)
                col = lax.broadcasted_iota(jnp.int32, s.shape, 1)
                s = jnp.where(col <= row, s, NEG_BIG)
            m_new = jnp.maximum(m, jnp.max(s, axis=-1, keepdims=True))
            alpha = jnp.exp(m - m_new)
            p = jnp.exp(s - m_new).astype(BF16)
            acc_ref[...] = alpha * acc_ref[...] + _dot(p, vext)
            new.append(m_new)
        return tuple(new)

    m0 = jnp.full((tq, 1), NEG_BIG, F32)
    carry = lax.fori_loop(0, qi, lambda j, c: block(j, c, False), (m0, m0))
    block(qi, carry, True)

    a1 = acc1_ref[...]
    a2 = acc2_ref[...]
    o1 = a1[:, :DA_DV] / a1[:, DA_DV:DA_DV + 1]
    o2 = a2[:, :DA_DV] / a2[:, DA_DV:DA_DV + 1]
    lp = lam_ref[...]
    lam = (jnp.exp(jnp.sum(lp[0:1] * lp[1:2], axis=-1, keepdims=True))
           - jnp.exp(jnp.sum(lp[2:3] * lp[3:4], axis=-1, keepdims=True)) + lp[4:5, 0:1])
    post = lp[4:5, 1:2]
    o = o1 - lam * o2
    ms = jnp.mean(o * o, axis=-1, keepdims=True)
    o_ref[0] = (o * lax.rsqrt(ms + NORM_EPS) * gsub_ref[...] * post).astype(o_ref.dtype)


def _diff_attn(q, k, v, lam_row, gsub, batch, seq):
    tq = min(TQ_ATTN, seq)
    tk = tq
    q3 = q.reshape(batch, seq, DA_WIDTH)
    k3 = k.reshape(batch, seq, DA_WIDTH)
    v3 = v.reshape(batch, seq, DA_WIDTH)
    out = pl.pallas_call(
        functools.partial(_attn_kernel, tq=tq, tk=tk),
        out_shape=jax.ShapeDtypeStruct((batch, seq, DA_WIDTH), BF16),
        grid=(batch, DA_HEADS, seq // tq),
        in_specs=[pl.BlockSpec((1, tq, DA_DV), lambda b, h, i: (b, i, h)),
                  pl.BlockSpec((1, seq, DA_DV), lambda b, h, i: (b, 0, h)),
                  pl.BlockSpec((1, seq, DA_DV), lambda b, h, i: (b, 0, h)),
                  _const_spec(lam_row.shape), _const_spec(gsub.shape)],
        out_specs=pl.BlockSpec((1, tq, DA_DV), lambda b, h, i: (b, i, h)),
        scratch_shapes=[pltpu.VMEM((tq, 2 * LANES), F32), pltpu.VMEM((tq, 2 * LANES), F32)],
        compiler_params=_cparams(3),
        name="diff_attn",
    )(q3, k3, v3, lam_row, gsub)
    return out.reshape(batch * seq, DA_WIDTH)


def _ssd_kernel(z_ref, xbc_ref, small_ref, cw_ref, cb_ref, dtb_ref, alog_ref, dskip_ref, gn_ref,
                tri_ref, exp_ref, y_ref, ext_ref, state_ref, *, ts):
    L = SSD_CHUNK
    N = SSD_DSTATE
    gw = SSD_DINNER // SSD_GROUPS
    rpg = SSD_HEADS // SSD_GROUPS
    t = pl.program_id(1)

    @pl.when(t == 0)
    def _():
        state_ref[...] = jnp.zeros_like(state_ref)
        ext_ref[0:SUBLANES, :] = jnp.zeros((SUBLANES, SSD_CONV_DIM), F32)

    ext_ref[SUBLANES:SUBLANES + ts, :] = xbc_ref[0]
    conv = cb_ref[...] + cw_ref[0:1, :] * ext_ref[SUBLANES - 3:SUBLANES - 3 + ts, :]
    for j in range(1, SSD_CONV):
        conv = conv + cw_ref[j:j + 1, :] * ext_ref[SUBLANES - 3 + j:SUBLANES - 3 + j + ts, :]
    ext_ref[0:SUBLANES, :] = ext_ref[ts:ts + SUBLANES, :]
    act = _silu(conv)

    a_neg = -jnp.exp(alog_ref[...])
    tri = tri_ref[...]
    expand = exp_ref[...]
    row_i = lax.broadcasted_iota(jnp.int32, (L, L), 0)
    col_i = lax.broadcasted_iota(jnp.int32, (L, L), 1)
    causal = col_i <= row_i

    for c in range(ts // L):
        r0 = c * L
        xs = act[r0:r0 + L, 0:SSD_DINNER]
        bm = act[r0:r0 + L, SSD_DINNER:SSD_DINNER + SSD_GROUPS * N]
        cm = act[r0:r0 + L, SSD_DINNER + SSD_GROUPS * N:]
        dt = _softplus(small_ref[0, r0:r0 + L, :] + dtb_ref[...])
        cum = _dot_exact_lhs(tri, dt * a_neg)
        cum_t = cum.T
        ecum = jnp.exp(cum)
        cum_last = cum[L - 1:L, :]
        dte = jnp.exp(cum_last - cum)
        dt_e = _dot_exact_rhs(dt, expand)
        ecum_e = _dot_exact_rhs(ecum, expand)
        dte_e = _dot_exact_rhs(dte, expand)
        xdt = xs * dt_e
        xdt_b = xdt.astype(BF16)
        xend_b = (xdt * dte_e).astype(BF16)
        chunk_decay = ecum_e[L - 1:L, :]
        ys = []
        for g in range(SSD_GROUPS):
            bg = bm[:, g * N:(g + 1) * N].astype(BF16)
            cg = cm[:, g * N:(g + 1) * N].astype(BF16)
            cb = _dot_nt(cg, bg)
            for r in range(rpg):
                hh = g * rpg + r
                seg = cum[:, hh:hh + 1] - cum_t[hh:hh + 1, :]
                decay = jnp.exp(jnp.where(causal, seg, NEG_BIG))
                m_h = (cb * decay).astype(BF16)
                ys.append(_dot(m_h, xdt_b[:, hh * SSD_HEADDIM:(hh + 1) * SSD_HEADDIM]))
            st = state_ref[g]
            y_off = _dot(cg, st.astype(BF16)) * ecum_e[:, g * gw:(g + 1) * gw]
            new_st = _dot_tn(bg, xend_b[:, g * gw:(g + 1) * gw])
            state_ref[g] = chunk_decay[:, g * gw:(g + 1) * gw] * st + new_st
            ys.append(y_off)
        y_diag = jnp.concatenate([ys[0], ys[1], ys[2], ys[3], ys[5], ys[6], ys[7], ys[8]], axis=1)
        y_off = jnp.concatenate([ys[4], ys[9]], axis=1)
        y = y_diag + y_off + dskip_ref[...] * xs
        y = y * _silu(z_ref[0, r0:r0 + L, :])
        outs = []
        for g in range(SSD_GROUPS):
            yg = y[:, g * gw:(g + 1) * gw]
            ms = jnp.mean(yg * yg, axis=-1, keepdims=True)
            outs.append(yg * lax.rsqrt(ms + NORM_EPS))
        yn = jnp.concatenate(outs, axis=1) * gn_ref[...]
        y_ref[0, r0:r0 + L, :] = yn.astype(y_ref.dtype)


def _ssd(z, xbc, small, cw, cb, dtb, alog, dskip, gn, tri, expand, batch, seq):
    ts = min(TS_SSD, seq)
    gw = SSD_DINNER // SSD_GROUPS
    tile = lambda w: pl.BlockSpec((1, ts, w), lambda b, i: (b, i, 0))
    out = pl.pallas_call(
        functools.partial(_ssd_kernel, ts=ts),
        out_shape=jax.ShapeDtypeStruct((batch, seq, SSD_DINNER), BF16),
        grid=(batch, seq // ts),
        in_specs=[tile(SSD_DINNER), tile(SSD_CONV_DIM), tile(LANES)]
        + [_const_spec(a.shape) for a in (cw, cb, dtb, alog, dskip, gn, tri, expand)],
        out_specs=tile(SSD_DINNER),
        scratch_shapes=[pltpu.VMEM((ts + 2 * SUBLANES, SSD_CONV_DIM), F32),
                        pltpu.VMEM((SSD_GROUPS, SSD_DSTATE, gw), F32)],
        compiler_params=_cparams(2),
        name="ssd",
    )(z.reshape(batch, seq, -1), xbc.reshape(batch, seq, -1), small.reshape(batch, seq, -1),
      cw, cb, dtb, alog, dskip, gn, tri, expand)
    return out.reshape(batch * seq, SSD_DINNER)


def _gla_kernel(q_ref, k_ref, v_ref, g_ref, small_ref, w2_ref, bg_ref, gn_ref, tri_ref,
                o_ref, state_ref, *, ts):
    L = GLA_CHUNK
    blk = 2 * L
    qk_w = GLA_HEADS * GLA_DK
    t = pl.program_id(1)

    @pl.when(t == 0)
    def _():
        state_ref[...] = jnp.zeros_like(state_ref)

    lane = lax.broadcasted_iota(jnp.int32, (1, qk_w), 1)
    head_mask = [(lane >= h * GLA_DK) & (lane < (h + 1) * GLA_DK) for h in range(GLA_HEADS)]
    row_i = lax.broadcasted_iota(jnp.int32, (blk, blk), 0)
    col_i = lax.broadcasted_iota(jnp.int32, (blk, blk), 1)
    same_chunk = (row_i >= L) == (col_i >= L)
    tril = (col_i <= row_i) & same_chunk
    tri = tri_ref[...]

    for c in range(ts // blk):
        r0 = c * blk
        lr = small_ref[0, r0:r0 + blk, :].astype(BF16)
        gate = _dot(lr, w2_ref[...]) + bg_ref[...]
        gk = (jnp.minimum(gate, 0.0) - jnp.log1p(jnp.exp(-jnp.abs(gate)))) / GLA_GATE_NORMALIZER
        bcum = _dot_exact_lhs(tri, gk)
        q = q_ref[0, r0:r0 + blk, :] * (GLA_DK ** -0.5)
        k = k_ref[0, r0:r0 + blk, :]
        v = v_ref[0, r0:r0 + blk, :]
        q_dec = q * jnp.exp(bcum)
        k_inv = (k * jnp.exp(-bcum)).astype(BF16)
        last = [bcum[L - 1:L, :], bcum[blk - 1:blk, :]]
        in_first = lax.broadcasted_iota(jnp.int32, (blk, qk_w), 0) < L
        k_end = (k * jnp.exp(jnp.where(in_first, last[0], last[1]) - bcum)).astype(BF16)

        s0 = state_ref[...]
        prevs = [s0]
        st = s0
        for ci in range(2):
            full = _dot_tn(v[ci * L:(ci + 1) * L, :], k_end[ci * L:(ci + 1) * L, :])
            new = jnp.zeros_like(st)
            for h in range(GLA_HEADS):
                new = new + jnp.where(head_mask[h], full[h * GLA_DV:(h + 1) * GLA_DV, :], 0.0)
            st = jnp.exp(last[ci]) * st + new
            if ci == 0:
                prevs.append(st)
        state_ref[...] = st
        prev_b = [p.astype(BF16) for p in prevs]

        outs = []
        for h in range(GLA_HEADS):
            qh = jnp.where(head_mask[h], q_dec, 0.0).astype(BF16)
            att = jnp.where(tril, _dot_nt(qh, k_inv), 0.0).astype(BF16)
            o_h = _dot(att, v[:, h * GLA_DV:(h + 1) * GLA_DV])
            inter = jnp.concatenate([_dot_nt(qh[0:L, :], prev_b[0]), _dot_nt(qh[L:blk, :], prev_b[1])],
                                    axis=0)
            o_h = o_h + inter
            ms = jnp.mean(o_h * o_h, axis=-1, keepdims=True)
            outs.append(o_h * lax.rsqrt(ms + NORM_EPS))
        o = jnp.concatenate(outs, axis=1) * gn_ref[...] * _silu(g_ref[0, r0:r0 + blk, :])
        o_ref[0, r0:r0 + blk, :] = o.astype(o_ref.dtype)


def _gla(q, k, v, g, small, w2, bg, gn, tri, batch, seq):
    ts = min(TS_GLA, seq)
    qk_w = GLA_HEADS * GLA_DK
    tile = lambda w: pl.BlockSpec((1, ts, w), lambda b, i: (b, i, 0))
    out = pl.pallas_call(
        functools.partial(_gla_kernel, ts=ts),
        out_shape=jax.ShapeDtypeStruct((batch, seq, GLA_WIDTH), BF16),
        grid=(batch, seq // ts),
        in_specs=[tile(qk_w), tile(qk_w), tile(GLA_WIDTH), tile(GLA_WIDTH), tile(LANES)]
        + [_const_spec(a.shape) for a in (w2, bg, gn, tri)],
        out_specs=tile(GLA_WIDTH),
        scratch_shapes=[pltpu.VMEM((GLA_DV, qk_w), F32)],
        compiler_params=_cparams(2),
        name="gla",
    )(q.reshape(batch, seq, -1), k.reshape(batch, seq, -1), v.reshape(batch, seq, -1),
      g.reshape(batch, seq, -1), small.reshape(batch, seq, -1), w2, bg, gn, tri)
    return out.reshape(batch * seq, GLA_WIDTH)


FF_CHUNKS = ((0, 1024), (1024, 2048), (2048, D_FF))


def _post_kernel(h_ref, yda_ref, yssd_ref, ygla_ref, p_ref, wout_ref, fgain_ref, wg_ref, wu_ref, wd_ref,
                 wpg_ref, wpp_ref, o_ref):
    mix = jnp.concatenate([yda_ref[...], yssd_ref[...], ygla_ref[...]], axis=1)
    h1 = h_ref[...] + _dot(mix, wout_ref[...])
    ms = jnp.mean(h1 * h1, axis=-1, keepdims=True)
    f = (h1 * lax.rsqrt(ms + NORM_EPS) * fgain_ref[...]).astype(BF16)
    h2 = h1
    for c0, c1 in FF_CHUNKS:
        gate = _dot(f, wg_ref[:, c0:c1])
        up = _dot(f, wu_ref[:, c0:c1])
        h2 = h2 + _dot((_silu(gate) * up).astype(BF16), wd_ref[c0:c1, :])
    emb_gate = _sigmoid(_dot(h2.astype(BF16), wpg_ref[...]))
    emb = _dot(p_ref[...].astype(BF16), wpp_ref[...])
    o_ref[...] = h2 + emb_gate * emb


def _post(h, yda, yssd, ygla, p, wout, fgain, wg, wu, wd, wpg, wpp):
    t = h.shape[0]
    tm = min(TM_POST, t)
    row = lambda w: pl.BlockSpec((tm, w), lambda i: (i, 0))
    return pl.pallas_call(
        _post_kernel,
        out_shape=jax.ShapeDtypeStruct((t, D_MODEL), F32),
        grid=(t // tm,),
        in_specs=[row(D_MODEL), row(DA_WIDTH), row(SSD_DINNER), row(GLA_WIDTH), row(PLE_DIM)]
        + [_const_spec(a.shape) for a in (wout, fgain, wg, wu, wd, wpg, wpp)],
        out_specs=row(D_MODEL),
        compiler_params=_cparams(1),
        name="post",
    )(h, yda, yssd, ygla, p, wout, fgain, wg, wu, wd, wpg, wpp)


def _group_mean_matrix():
    w = DA_HEADS * 2 * DA_DQK
    g = np.arange(w) // DA_DQK
    return jnp.asarray((g[:, None] == g[None, :]).astype(np.float32) / DA_DQK, BF16)


def _tri_matrix(n, chunk):
    i = np.arange(n)
    m = (i[None, :] <= i[:, None]) & ((i[None, :] // chunk) == (i[:, None] // chunk))
    return jnp.asarray(m.astype(np.float32), BF16)


def _head_expand_matrix():
    m = np.zeros((LANES, SSD_DINNER), np.float32)
    for hh in range(SSD_HEADS):
        m[hh, hh * SSD_HEADDIM:(hh + 1) * SSD_HEADDIM] = 1.0
    return jnp.asarray(m, BF16)


def _pad_lanes(x, lane0):
    return jnp.pad(x, ((0, 0), (lane0, LANES - lane0 - x.shape[1])))


def kernel(x, p, positions, attn_norm, w_in, da_q_norm, da_k_norm, da_lambda_q1, da_lambda_k1,
           da_lambda_q2, da_lambda_k2, da_sub_norm, ssd_conv_w, ssd_conv_b, ssd_dt_bias, ssd_a_log,
           ssd_d, ssd_norm, gla_w_gate2, gla_b_gate, gla_norm, w_out, ffn_norm, w_ffn_gate, w_ffn_up,
           w_ffn_down, ple_w_proj, ple_w_gate):
    batch, seq, _ = x.shape
    depth = w_in.shape[0]
    t = batch * seq
    c_tab, s_tab = _rope_tables(positions)
    gmat = _group_mean_matrix()
    tri_ssd = _tri_matrix(SSD_CHUNK, SSD_CHUNK)
    tri_gla = _tri_matrix(2 * GLA_CHUNK, GLA_CHUNK)
    expand = _head_expand_matrix()

    qk_w = DA_HEADS * 2 * DA_DQK
    o_dt = 2 * qk_w + DA_WIDTH + SSD_DINNER + SSD_CONV_DIM
    o_gla = o_dt + SSD_HEADS
    gla_cols = 2 * GLA_HEADS * GLA_DK + 2 * GLA_WIDTH
    o_lr = o_gla + gla_cols

    h = x.reshape(t, D_MODEL)
    for i in range(depth):
        lambda_init = 0.8 - 0.6 * math.exp(-0.3 * i)
        wi = w_in[i]
        w1 = wi[:, :o_dt].astype(BF16)
        w2 = wi[:, o_gla:o_lr].astype(BF16)
        w3 = jnp.concatenate([wi[:, o_dt:o_gla], wi[:, o_lr:o_lr + GLA_GATE_RANK]], axis=1)
        w3 = _pad_lanes(w3, 0).astype(BF16)
        qg = jnp.tile(da_q_norm[i], 2 * DA_HEADS).reshape(1, qk_w)
        kg = jnp.tile(da_k_norm[i], 2 * DA_HEADS).reshape(1, qk_w)
        (q_da, k_da, v_da, z, xbc, small, gq, gk, gv, gg) = _in_proj(
            h, attn_norm[i].reshape(1, D_MODEL), w1, w2, w3, gmat, qg, kg, c_tab, s_tab)

        lam_vecs = jnp.stack([da_lambda_q1[i], da_lambda_k1[i], da_lambda_q2[i], da_lambda_k2[i]])
        lam_consts = jnp.asarray([[lambda_init, 1.0 - lambda_init]], F32)
        lam_row = jnp.concatenate([_pad_lanes(lam_vecs, 0), _pad_lanes(lam_consts, 0),
                                   jnp.zeros((SUBLANES - 5, LANES), F32)], axis=0)
        y_da = _diff_attn(q_da, k_da, v_da, lam_row, da_sub_norm[i].reshape(1, DA_DV), batch, seq)

        y_ssd = _ssd(z, xbc, small, ssd_conv_w[i], ssd_conv_b[i].reshape(1, -1),
                     _pad_lanes(ssd_dt_bias[i].reshape(1, -1), DT_LANE0),
                     _pad_lanes(ssd_a_log[i].reshape(1, -1), DT_LANE0),
                     jnp.repeat(ssd_d[i], SSD_HEADDIM).reshape(1, -1), ssd_norm[i].reshape(1, -1),
                     tri_ssd, expand, batch, seq)

        w_g2 = jnp.pad(gla_w_gate2[i], ((LR_LANE0, LANES - LR_LANE0 - GLA_GATE_RANK), (0, 0))).astype(BF16)
        y_gla = _gla(gq, gk, gv, gg, small, w_g2, gla_b_gate[i].reshape(1, -1),
                     jnp.tile(gla_norm[i], GLA_HEADS).reshape(1, -1), tri_gla, batch, seq)

        h = _post(h, y_da, y_ssd, y_gla, p[i].reshape(t, PLE_DIM), w_out[i].astype(BF16),
                  ffn_norm[i].reshape(1, D_MODEL), w_ffn_gate[i].astype(BF16), w_ffn_up[i].astype(BF16),
                  w_ffn_down[i].astype(BF16), ple_w_gate[i].astype(BF16), ple_w_proj[i].astype(BF16))
    return h.reshape(batch, seq, D_MODEL)
```

```python
import functools
import math

import jax
import jax.numpy as jnp
import numpy as np
from jax import lax
from jax.experimental import pallas as pl
from jax.experimental.pallas import tpu as pltpu

D_MODEL = 1024
PLE_DIM = 256
ROPE_THETA = 500000.0
NORM_EPS = 1e-6

DA_HEADS = 4
DA_DQK = 64
DA_DV = 2 * DA_DQK
DA_ROT = DA_DQK // 4
DA_WIDTH = DA_HEADS * DA_DV

SSD_HEADS = 8
SSD_HEADDIM = 64
SSD_DINNER = SSD_HEADS * SSD_HEADDIM
SSD_GROUPS = 2
SSD_DSTATE = 128
SSD_CONV = 4
SSD_CHUNK = 128
SSD_CONV_DIM = SSD_DINNER + 2 * SSD_GROUPS * SSD_DSTATE

GLA_HEADS = 4
GLA_DK = 64
GLA_DV = 128
GLA_GATE_RANK = 16
GLA_GATE_NORMALIZER = 16.0
GLA_CHUNK = 64
GLA_WIDTH = GLA_HEADS * GLA_DV

MIX_WIDTH = DA_WIDTH + SSD_DINNER + GLA_WIDTH
D_FF = -(-8 * D_MODEL // (3 * 256)) * 256

LANES = 128
SUBLANES = 8
VMEM_LIMIT = 56 * 1024 * 1024

DT_LANE0 = 0
LR_LANE0 = SSD_HEADS

TM_PROJ = 512
TM_POST = 512
TQ_ATTN = 1024
TK_ATTN = 512
ATTN_DIAG_STRIP = 256
TS_SSD = 512
TS_GLA = 512
ROPE_BLK = 2048

NEG_BIG = -1e30
LOG2E = math.log2(math.e)
SCORE_NORM = DA_DQK ** 0.5
MAX_UNSHIFTED_SCORE = 60.0
F32 = jnp.float32
BF16 = jnp.bfloat16


def _dot(a, b):
    return jnp.dot(a, b, preferred_element_type=F32)


def _dot_nt(a, b):
    return lax.dot_general(a, b, (((1,), (1,)), ((), ())), preferred_element_type=F32)


def _dot_tn(a, b):
    return lax.dot_general(a, b, (((0,), (0,)), ((), ())), preferred_element_type=F32)


def _split3(x):
    hi = x.astype(BF16)
    r1 = x - hi.astype(F32)
    mid = r1.astype(BF16)
    lo = (r1 - mid.astype(F32)).astype(BF16)
    return hi, mid, lo


def _dot_exact_lhs(m_bf16, x):
    hi, mid, lo = _split3(x)
    return _dot(m_bf16, hi) + _dot(m_bf16, mid) + _dot(m_bf16, lo)


def _dot_exact_rhs(x, e_bf16):
    hi, mid, lo = _split3(x)
    return _dot(hi, e_bf16) + _dot(mid, e_bf16) + _dot(lo, e_bf16)


def _sigmoid(x):
    return 1.0 / (1.0 + jnp.exp(-x))


def _silu(x):
    return x * _sigmoid(x)


def _softplus(x):
    return jnp.maximum(x, 0.0) + jnp.log1p(jnp.exp(-jnp.abs(x)))


def _cparams(n_axes):
    return pltpu.CompilerParams(dimension_semantics=("arbitrary",) * n_axes,
                                vmem_limit_bytes=VMEM_LIMIT)


def _const_spec(shape):
    nd = len(shape)
    return pl.BlockSpec(shape, lambda *_: (0,) * nd, pipeline_mode=pl.Buffered(1))


def _rope_kernel(pos_ref, invf_ref, cos_ref, sin_ref):
    ang = pos_ref[...].astype(F32) * invf_ref[...]
    cos_ref[...] = jnp.cos(ang)
    sin_ref[...] = jnp.sin(ang)


def _rope_tables(positions):
    t = positions.size
    half = DA_ROT // 2
    blk = min(ROPE_BLK, t)
    inv_freq = (ROPE_THETA ** (-jnp.arange(0, DA_ROT, 2, dtype=F32) / DA_ROT)).reshape(half, 1)
    cos8, sin8 = pl.pallas_call(
        _rope_kernel,
        out_shape=(jax.ShapeDtypeStruct((half, t), F32),) * 2,
        grid=(t // blk,),
        in_specs=[pl.BlockSpec((1, blk), lambda i: (0, i)),
                  pl.BlockSpec((half, 1), lambda i: (0, 0))],
        out_specs=(pl.BlockSpec((half, blk), lambda i: (0, i)),) * 2,
        compiler_params=_cparams(1),
        name="rope_tables",
    )(positions.reshape(1, t), inv_freq)
    c = cos8.T
    s = sin8.T
    rest = DA_DQK - DA_ROT
    c64 = jnp.concatenate([c, c, jnp.ones((t, rest), F32)], axis=1)
    s64 = jnp.concatenate([-s, s, jnp.zeros((t, rest), F32)], axis=1)
    reps = LANES // DA_DQK
    return jnp.tile(c64, (1, reps)), jnp.tile(s64, (1, reps))


def _in_proj_kernel(h_ref, gain_ref, w1_ref, w2_ref, w3_ref, gmat_ref, qg_ref, kg_ref, c_ref, s_ref,
                    cw_ref, cb_ref,
                    q_out, k_out, v_out, z_out, xs_out, bc_out, small_out, gq_out, gk_out, gv_out, gg_out,
                    ext_ref, *, tiles_per_seq):
    h = h_ref[...]
    ms = jnp.mean(h * h, axis=-1, keepdims=True)
    a = (h * lax.rsqrt(ms + NORM_EPS) * gain_ref[...]).astype(BF16)
    tm = h.shape[0]
    qk_w = DA_HEADS * 2 * DA_DQK
    n_slab = qk_w // LANES
    half = DA_ROT // 2

    c_tab = c_ref[...]
    s_tab = s_ref[...]
    lane = lax.broadcasted_iota(jnp.int32, (tm, LANES), 1) & (DA_DQK - 1)
    first_half = lane < half

    def norm_rope(raw, g_ref, scale):
        msq = _dot((raw * raw).astype(BF16), gmat_ref[...])
        n = raw * lax.rsqrt(msq + NORM_EPS) * g_ref[...]
        outs = []
        for i in range(n_slab):
            x = n[:, i * LANES:(i + 1) * LANES]
            partner = jnp.where(first_half, pltpu.roll(x, LANES - half, 1), pltpu.roll(x, half, 1))
            outs.append((x * c_tab + partner * s_tab) * scale)
        return jnp.concatenate(outs, axis=1).astype(BF16)

    q_out[...] = norm_rope(_dot(a, w1_ref[:, 0:qk_w]), qg_ref, DA_DQK ** -0.5 * LOG2E)
    k_out[...] = norm_rope(_dot(a, w1_ref[:, qk_w:2 * qk_w]), kg_ref, 1.0)
    o = 2 * qk_w
    v_out[...] = _dot(a, w1_ref[:, o:o + DA_WIDTH]).astype(BF16)
    o += DA_WIDTH
    z_out[...] = _dot(a, w1_ref[:, o:o + SSD_DINNER])
    o += SSD_DINNER

    @pl.when(pl.program_id(0) % tiles_per_seq == 0)
    def _():
        ext_ref[0:SUBLANES, :] = jnp.zeros((SUBLANES, SSD_CONV_DIM), F32)

    ext_ref[SUBLANES:SUBLANES + tm, :] = _dot(a, w1_ref[:, o:o + SSD_CONV_DIM])
    first = SUBLANES - (SSD_CONV - 1)
    conv = cb_ref[...] + cw_ref[0:1, :] * ext_ref[first:first + tm, :]
    for j in range(1, SSD_CONV):
        conv = conv + cw_ref[j:j + 1, :] * ext_ref[first + j:first + j + tm, :]
    ext_ref[0:SUBLANES, :] = ext_ref[tm:tm + SUBLANES, :]
    act = _silu(conv)
    xs_out[...] = act[:, 0:SSD_DINNER]
    bc_out[...] = act[:, SSD_DINNER:].astype(BF16)
    small_out[...] = _dot(a, w3_ref[...])
    gqk = GLA_HEADS * GLA_DK
    gq_out[...] = _dot(a, w2_ref[:, 0:gqk])
    gk_out[...] = _dot(a, w2_ref[:, gqk:2 * gqk])
    gv_out[...] = _dot(a, w2_ref[:, 2 * gqk:2 * gqk + GLA_WIDTH]).astype(BF16)
    gg_out[...] = _dot(a, w2_ref[:, 2 * gqk + GLA_WIDTH:2 * gqk + 2 * GLA_WIDTH])


def _in_proj(h, gain, w1, w2, w3, gmat, qg, kg, c_tab, s_tab, cw, cb, seq):
    t = h.shape[0]
    tm = min(TM_PROJ, seq)
    qk_w = DA_HEADS * 2 * DA_DQK
    gqk = GLA_HEADS * GLA_DK
    widths = [(qk_w, BF16), (qk_w, BF16), (DA_WIDTH, BF16), (SSD_DINNER, F32), (SSD_DINNER, F32),
              (2 * SSD_GROUPS * SSD_DSTATE, BF16), (LANES, F32), (gqk, F32), (gqk, F32),
              (GLA_WIDTH, BF16), (GLA_WIDTH, F32)]
    row = lambda w: pl.BlockSpec((tm, w), lambda i: (i, 0))
    consts = (gain, w1, w2, w3, gmat, qg, kg)
    return pl.pallas_call(
        functools.partial(_in_proj_kernel, tiles_per_seq=seq // tm),
        out_shape=tuple(jax.ShapeDtypeStruct((t, w), d) for w, d in widths),
        grid=(t // tm,),
        in_specs=[row(D_MODEL)] + [_const_spec(a.shape) for a in consts] + [row(LANES), row(LANES)]
        + [_const_spec(cw.shape), _const_spec(cb.shape)],
        out_specs=tuple(row(w) for w, _ in widths),
        scratch_shapes=[pltpu.VMEM((tm + 2 * SUBLANES, SSD_CONV_DIM), F32)],
        compiler_params=_cparams(1),
        name="in_proj",
    )(h, *consts, c_tab, s_tab, cw, cb)


def _attn_kernel(q_ref, k_ref, v_ref, lam_ref, gsub_ref, o_ref, acc1_ref, acc2_ref, *, tq, tk, online):
    qi = pl.program_id(2)
    q = q_ref[0]
    lane = lax.broadcasted_iota(jnp.int32, q.shape, 1)
    zero = jnp.zeros_like(q)
    q1 = jnp.where(lane < DA_DQK, q, zero)
    q2 = jnp.where(lane >= DA_DQK, q, zero)
    ones_col = (lax.broadcasted_iota(jnp.int32, (tk, LANES), 1) == 0).astype(BF16)
    acc1_ref[...] = jnp.zeros_like(acc1_ref)
    acc2_ref[...] = jnp.zeros_like(acc2_ref)

    kv_per_q = tq // tk
    n_full = qi * kv_per_q

    def kv_block(j, n=tk):
        start = pl.multiple_of(j * tk, tk)
        kj = k_ref[0, pl.ds(start, n), :]
        ones = ones_col if n == tk else jnp.concatenate([ones_col] * (n // tk), axis=0)
        return kj, jnp.concatenate([v_ref[0, pl.ds(start, n), :], ones], axis=1)

    if online:
        def block(j, carry, col0):
            kj, vext = kv_block(j)
            new = []
            for qc, m, acc_ref in ((q1, carry[0], acc1_ref), (q2, carry[1], acc2_ref)):
                s = _dot_nt(qc, kj)
                if col0 is not None:
                    row = lax.broadcasted_iota(jnp.int32, s.shape, 0)
                    col = lax.broadcasted_iota(jnp.int32, s.shape, 1) + col0
                    s = jnp.where(col <= row, s, NEG_BIG)
                m_new = jnp.maximum(m, jnp.max(s, axis=-1, keepdims=True))
                alpha = jnp.exp2(m - m_new)
                p = jnp.exp2(s - m_new).astype(BF16)
                acc_ref[...] = alpha * acc_ref[...] + _dot(p, vext)
                new.append(m_new)
            return tuple(new)

        m0 = jnp.full((tq, 1), NEG_BIG, F32)
        carry = lax.fori_loop(0, n_full, lambda j, c: block(j, c, None), (m0, m0))
        for d in range(kv_per_q):
            carry = block(n_full + d, carry, d * tk)
    else:
        def full_block(j, c):
            kj, vext = kv_block(j)
            for qc, acc_ref in ((q1, acc1_ref), (q2, acc2_ref)):
                p = jnp.exp2(_dot_nt(qc, kj)).astype(BF16)
                acc_ref[...] += _dot(p, vext)
            return c

        lax.fori_loop(0, n_full, full_block, 0)
        kd, vd = kv_block(n_full, tq)
        rs = min(ATTN_DIAG_STRIP, tq)
        for r in range(tq // rs):
            r0, nk = r * rs, (r + 1) * rs
            row = lax.broadcasted_iota(jnp.int32, (rs, nk), 0) + r0
            col = lax.broadcasted_iota(jnp.int32, (rs, nk), 1)
            keep = col <= row
            for qc, acc_ref in ((q1, acc1_ref), (q2, acc2_ref)):
                p = jnp.where(keep, jnp.exp2(_dot_nt(qc[r0:r0 + rs], kd[0:nk])), 0.0).astype(BF16)
                acc_ref[r0:r0 + rs, :] += _dot(p, vd[0:nk])

    a1 = acc1_ref[...]
    a2 = acc2_ref[...]
    o1 = a1[:, :DA_DV] / a1[:, DA_DV:DA_DV + 1]
    o2 = a2[:, :DA_DV] / a2[:, DA_DV:DA_DV + 1]
    lp = lam_ref[...]
    lam = (jnp.exp(jnp.sum(lp[0:1] * lp[1:2], axis=-1, keepdims=True))
           - jnp.exp(jnp.sum(lp[2:3] * lp[3:4], axis=-1, keepdims=True)) + lp[4:5, 0:1])
    post = lp[4:5, 1:2]
    o = o1 - lam * o2
    ms = jnp.mean(o * o, axis=-1, keepdims=True)
    o_ref[0] = (o * lax.rsqrt(ms + NORM_EPS) * gsub_ref[...] * post).astype(o_ref.dtype)


def _diff_attn(q, k, v, lam_row, gsub, batch, seq, online):
    tq = min(TQ_ATTN, seq)
    tk = min(TK_ATTN, tq)
    q3 = q.reshape(batch, seq, DA_WIDTH)
    k3 = k.reshape(batch, seq, DA_WIDTH)
    v3 = v.reshape(batch, seq, DA_WIDTH)
    out = pl.pallas_call(
        functools.partial(_attn_kernel, tq=tq, tk=tk, online=online),
        out_shape=jax.ShapeDtypeStruct((batch, seq, DA_WIDTH), BF16),
        grid=(batch, DA_HEADS, seq // tq),
        in_specs=[pl.BlockSpec((1, tq, DA_DV), lambda b, h, i: (b, i, h)),
                  pl.BlockSpec((1, seq, DA_DV), lambda b, h, i: (b, 0, h)),
                  pl.BlockSpec((1, seq, DA_DV), lambda b, h, i: (b, 0, h)),
                  _const_spec(lam_row.shape), _const_spec(gsub.shape)],
        out_specs=pl.BlockSpec((1, tq, DA_DV), lambda b, h, i: (b, i, h)),
        scratch_shapes=[pltpu.VMEM((tq, 2 * LANES), F32), pltpu.VMEM((tq, 2 * LANES), F32)],
        compiler_params=_cparams(3),
        name="diff_attn_online" if online else "diff_attn",
    )(q3, k3, v3, lam_row, gsub)
    return out.reshape(batch * seq, DA_WIDTH)


def _ssd_kernel(z_ref, xs_ref, bc_ref, small_ref, dtb_ref, alog_ref, dskip_ref, gn_ref,
                tri_ref, exp_ref, y_ref, state_ref, *, ts):
    L = SSD_CHUNK
    N = SSD_DSTATE
    H = SSD_HEADS
    gw = SSD_DINNER // SSD_GROUPS
    rpg = H // SSD_GROUPS
    t = pl.program_id(1)

    @pl.when(t == 0)
    def _():
        state_ref[...] = jnp.zeros_like(state_ref)

    a_neg = -jnp.exp(alog_ref[...])
    tri = tri_ref[...]
    expand = exp_ref[...]
    row_i = lax.broadcasted_iota(jnp.int32, (L, L), 0)
    col_i = lax.broadcasted_iota(jnp.int32, (L, L), 1)
    causal = col_i <= row_i
    head_lane = lax.broadcasted_iota(jnp.int32, (L, LANES), 1) < H
    glane = lax.broadcasted_iota(jnp.int32, (1, gw), 1)
    head_mask = [(glane >= r * SSD_HEADDIM) & (glane < (r + 1) * SSD_HEADDIM) for r in range(rpg)]

    for c in range(ts // L):
        r0 = c * L
        xs = xs_ref[0, r0:r0 + L, :]
        bm = bc_ref[0, r0:r0 + L, 0:SSD_GROUPS * N]
        cm = bc_ref[0, r0:r0 + L, SSD_GROUPS * N:]
        dt = _softplus(small_ref[0, r0:r0 + L, :] + dtb_ref[...])
        cum = _dot_exact_lhs(tri, dt * a_neg)
        cum_t = cum.T
        cum_last = cum[L - 1:L, :]
        packed = (jnp.where(head_lane, dt, 0.0)
                  + pltpu.roll(jnp.where(head_lane, jnp.exp(cum), 0.0), H, 1)
                  + pltpu.roll(jnp.where(head_lane, jnp.exp(cum_last - cum), 0.0), 2 * H, 1))
        hi = packed.astype(BF16)
        lo = (packed - hi.astype(F32)).astype(BF16)
        factors = _dot(hi, expand) + _dot(lo, expand)
        dt_e = factors[:, 0:SSD_DINNER]
        ecum_e = factors[:, SSD_DINNER:2 * SSD_DINNER]
        dte_e = factors[:, 2 * SSD_DINNER:]
        xdt = xs * dt_e
        xdt_b = xdt.astype(BF16)
        xend_b = (xdt * dte_e).astype(BF16)
        chunk_decay = ecum_e[L - 1:L, :]
        y_parts = []
        for g in range(SSD_GROUPS):
            bg = bm[:, g * N:(g + 1) * N]
            cg = cm[:, g * N:(g + 1) * N]
            cb = _dot_nt(cg, bg)
            m_heads = []
            for r in range(rpg):
                hh = g * rpg + r
                seg = cum[:, hh:hh + 1] - cum_t[hh:hh + 1, :]
                decay = jnp.exp(jnp.where(causal, seg, NEG_BIG))
                m_heads.append((cb * decay).astype(BF16))
            xg = xdt_b[:, g * gw:(g + 1) * gw]
            x_bd = jnp.concatenate([jnp.where(head_mask[r], xg, jnp.zeros_like(xg)) for r in range(rpg)],
                                   axis=0)
            y_diag = _dot(jnp.concatenate(m_heads, axis=1), x_bd)
            st = state_ref[g]
            y_off = _dot(cg, st.astype(BF16)) * ecum_e[:, g * gw:(g + 1) * gw]
            new_st = _dot_tn(bg, xend_b[:, g * gw:(g + 1) * gw])
            state_ref[g] = chunk_decay[:, g * gw:(g + 1) * gw] * st + new_st
            y_parts.append(y_diag + y_off)
        y = jnp.concatenate(y_parts, axis=1) + dskip_ref[...] * xs
        y = y * _silu(z_ref[0, r0:r0 + L, :])
        outs = []
        for g in range(SSD_GROUPS):
            yg = y[:, g * gw:(g + 1) * gw]
            ms = jnp.mean(yg * yg, axis=-1, keepdims=True)
            outs.append(yg * lax.rsqrt(ms + NORM_EPS))
        yn = jnp.concatenate(outs, axis=1) * gn_ref[...]
        y_ref[0, r0:r0 + L, :] = yn.astype(y_ref.dtype)


def _ssd(z, xs, bc, small, dtb, alog, dskip, gn, tri, expand, batch, seq):
    ts = min(TS_SSD, seq)
    gw = SSD_DINNER // SSD_GROUPS
    tile = lambda w: pl.BlockSpec((1, ts, w), lambda b, i: (b, i, 0))
    out = pl.pallas_call(
        functools.partial(_ssd_kernel, ts=ts),
        out_shape=jax.ShapeDtypeStruct((batch, seq, SSD_DINNER), BF16),
        grid=(batch, seq // ts),
        in_specs=[tile(SSD_DINNER), tile(SSD_DINNER), tile(2 * SSD_GROUPS * SSD_DSTATE), tile(LANES)]
        + [_const_spec(a.shape) for a in (dtb, alog, dskip, gn, tri, expand)],
        out_specs=tile(SSD_DINNER),
        scratch_shapes=[pltpu.VMEM((SSD_GROUPS, SSD_DSTATE, gw), F32)],
        compiler_params=_cparams(2),
        name="ssd",
    )(z.reshape(batch, seq, -1), xs.reshape(batch, seq, -1), bc.reshape(batch, seq, -1),
      small.reshape(batch, seq, -1), dtb, alog, dskip, gn, tri, expand)
    return out.reshape(batch * seq, SSD_DINNER)


def _gla_kernel(q_ref, k_ref, v_ref, g_ref, small_ref, w2_ref, bg_ref, gn_ref, tri_ref,
                o_ref, state_ref, *, ts):
    L = GLA_CHUNK
    blk = 2 * L
    qk_w = GLA_HEADS * GLA_DK
    t = pl.program_id(1)

    @pl.when(t == 0)
    def _():
        state_ref[...] = jnp.zeros_like(state_ref)

    lane = lax.broadcasted_iota(jnp.int32, (1, qk_w), 1)
    head_mask = [(lane >= h * GLA_DK) & (lane < (h + 1) * GLA_DK) for h in range(GLA_HEADS)]
    row_i = lax.broadcasted_iota(jnp.int32, (GLA_HEADS * blk, blk), 0)
    col_i = lax.broadcasted_iota(jnp.int32, (GLA_HEADS * blk, blk), 1)
    same_chunk = (row_i >= GLA_HEADS * L) == (col_i >= L)
    tril = ((col_i & (L - 1)) <= (row_i & (L - 1))) & same_chunk
    tri = tri_ref[...]

    n_chunk = ts // L
    lr = small_ref[0].astype(BF16)
    gate = _dot(lr, w2_ref[...]) + bg_ref[...]
    gk = (jnp.minimum(gate, 0.0) - jnp.log1p(jnp.exp(-jnp.abs(gate)))) / GLA_GATE_NORMALIZER
    bcum = jnp.concatenate([_dot_exact_lhs(tri, gk[b * blk:(b + 1) * blk]) for b in range(ts // blk)],
                           axis=0)
    k = k_ref[0]
    v = v_ref[0]
    q_dec = q_ref[0] * (GLA_DK ** -0.5) * jnp.exp(bcum)
    k_inv = (k * jnp.exp(-bcum)).astype(BF16)
    lasts = [bcum[c * L + L - 1:c * L + L, :] for c in range(n_chunk)]
    last_rows = jnp.concatenate([jnp.broadcast_to(l, (L, qk_w)) for l in lasts], axis=0)
    k_end = (k * jnp.exp(last_rows - bcum)).astype(BF16)

    news = []
    for c in range(n_chunk):
        full = _dot_tn(v[c * L:(c + 1) * L, :], k_end[c * L:(c + 1) * L, :])
        new = jnp.where(head_mask[0], full[0:GLA_DV, :], 0.0)
        for h in range(1, GLA_HEADS):
            new = new + jnp.where(head_mask[h], full[h * GLA_DV:(h + 1) * GLA_DV, :], 0.0)
        news.append(new)
    st = state_ref[...]
    prev_b = []
    for c in range(n_chunk):
        prev_b.append(st.astype(BF16))
        st = jnp.exp(lasts[c]) * st + news[c]
    state_ref[...] = st

    hl = GLA_HEADS * L
    for b in range(ts // blk):
        r0 = b * blk
        q4 = jnp.concatenate(
            [jnp.where(head_mask[h], q_dec[r0 + ci * L:r0 + (ci + 1) * L], 0.0)
             for ci in range(2) for h in range(GLA_HEADS)], axis=0).astype(BF16)
        att = jnp.where(tril, _dot_nt(q4, k_inv[r0:r0 + blk]), 0.0).astype(BF16)
        inter = [_dot_nt(q4[ci * hl:(ci + 1) * hl], prev_b[2 * b + ci]) for ci in range(2)]
        outs = []
        for h in range(GLA_HEADS):
            att_h = jnp.concatenate([att[ci * hl + h * L:ci * hl + (h + 1) * L] for ci in range(2)], axis=0)
            inter_h = jnp.concatenate([inter[ci][h * L:(h + 1) * L] for ci in range(2)], axis=0)
            o_h = _dot(att_h, v[r0:r0 + blk, h * GLA_DV:(h + 1) * GLA_DV]) + inter_h
            ms = jnp.mean(o_h * o_h, axis=-1, keepdims=True)
            outs.append(o_h * lax.rsqrt(ms + NORM_EPS))
        o = jnp.concatenate(outs, axis=1) * gn_ref[...] * _silu(g_ref[0, r0:r0 + blk, :])
        o_ref[0, r0:r0 + blk, :] = o.astype(o_ref.dtype)


def _gla(q, k, v, g, small, w2, bg, gn, tri, batch, seq):
    ts = min(TS_GLA, seq)
    qk_w = GLA_HEADS * GLA_DK
    tile = lambda w: pl.BlockSpec((1, ts, w), lambda b, i: (b, i, 0))
    out = pl.pallas_call(
        functools.partial(_gla_kernel, ts=ts),
        out_shape=jax.ShapeDtypeStruct((batch, seq, GLA_WIDTH), BF16),
        grid=(batch, seq // ts),
        in_specs=[tile(qk_w), tile(qk_w), tile(GLA_WIDTH), tile(GLA_WIDTH), tile(LANES)]
        + [_const_spec(a.shape) for a in (w2, bg, gn, tri)],
        out_specs=tile(GLA_WIDTH),
        scratch_shapes=[pltpu.VMEM((GLA_DV, qk_w), F32)],
        compiler_params=_cparams(2),
        name="gla",
    )(q.reshape(batch, seq, -1), k.reshape(batch, seq, -1), v.reshape(batch, seq, -1),
      g.reshape(batch, seq, -1), small.reshape(batch, seq, -1), w2, bg, gn, tri)
    return out.reshape(batch * seq, GLA_WIDTH)


FF_CHUNKS = ((0, 1024), (1024, 2048), (2048, D_FF))


def _post_kernel(h_ref, yda_ref, yssd_ref, ygla_ref, p_ref, wout_ref, fgain_ref, wg_ref, wu_ref, wd_ref,
                 wpg_ref, wpp_ref, o_ref):
    mix = jnp.concatenate([yda_ref[...], yssd_ref[...], ygla_ref[...]], axis=1)
    h1 = h_ref[...] + _dot(mix, wout_ref[...])
    ms = jnp.mean(h1 * h1, axis=-1, keepdims=True)
    f = (h1 * lax.rsqrt(ms + NORM_EPS) * fgain_ref[...]).astype(BF16)
    h2 = h1
    for c0, c1 in FF_CHUNKS:
        gate = _dot(f, wg_ref[:, c0:c1])
        up = _dot(f, wu_ref[:, c0:c1])
        h2 = h2 + _dot((_silu(gate) * up).astype(BF16), wd_ref[c0:c1, :])
    emb_gate = _sigmoid(_dot(h2.astype(BF16), wpg_ref[...]))
    emb = _dot(p_ref[...].astype(BF16), wpp_ref[...])
    o_ref[...] = h2 + emb_gate * emb


def _post(h, yda, yssd, ygla, p, wout, fgain, wg, wu, wd, wpg, wpp):
    t = h.shape[0]
    tm = min(TM_POST, t)
    row = lambda w: pl.BlockSpec((tm, w), lambda i: (i, 0))
    return pl.pallas_call(
        _post_kernel,
        out_shape=jax.ShapeDtypeStruct((t, D_MODEL), F32),
        grid=(t // tm,),
        in_specs=[row(D_MODEL), row(DA_WIDTH), row(SSD_DINNER), row(GLA_WIDTH), row(PLE_DIM)]
        + [_const_spec(a.shape) for a in (wout, fgain, wg, wu, wd, wpg, wpp)],
        out_specs=row(D_MODEL),
        compiler_params=_cparams(1),
        name="post",
    )(h, yda, yssd, ygla, p, wout, fgain, wg, wu, wd, wpg, wpp)


def _group_mean_matrix():
    w = DA_HEADS * 2 * DA_DQK
    g = np.arange(w) // DA_DQK
    return jnp.asarray((g[:, None] == g[None, :]).astype(np.float32) / DA_DQK, BF16)


def _tri_matrix(n, chunk):
    i = np.arange(n)
    m = (i[None, :] <= i[:, None]) & ((i[None, :] // chunk) == (i[:, None] // chunk))
    return jnp.asarray(m.astype(np.float32), BF16)


def _head_expand_matrix():
    n_factors = 3
    m = np.zeros((LANES, n_factors * SSD_DINNER), np.float32)
    for f in range(n_factors):
        for hh in range(SSD_HEADS):
            c0 = f * SSD_DINNER + hh * SSD_HEADDIM
            m[f * SSD_HEADS + hh, c0:c0 + SSD_HEADDIM] = 1.0
    return jnp.asarray(m, BF16)


def _pad_lanes(x, lane0):
    return jnp.pad(x, ((0, 0), (lane0, LANES - lane0 - x.shape[1])))


def kernel(x, p, positions, attn_norm, w_in, da_q_norm, da_k_norm, da_lambda_q1, da_lambda_k1,
           da_lambda_q2, da_lambda_k2, da_sub_norm, ssd_conv_w, ssd_conv_b, ssd_dt_bias, ssd_a_log,
           ssd_d, ssd_norm, gla_w_gate2, gla_b_gate, gla_norm, w_out, ffn_norm, w_ffn_gate, w_ffn_up,
           w_ffn_down, ple_w_proj, ple_w_gate):
    batch, seq, _ = x.shape
    depth = w_in.shape[0]
    t = batch * seq
    c_tab, s_tab = _rope_tables(positions)
    gmat = _group_mean_matrix()
    tri_ssd = _tri_matrix(SSD_CHUNK, SSD_CHUNK)
    tri_gla = _tri_matrix(2 * GLA_CHUNK, GLA_CHUNK)
    expand = _head_expand_matrix()

    qk_w = DA_HEADS * 2 * DA_DQK
    o_dt = 2 * qk_w + DA_WIDTH + SSD_DINNER + SSD_CONV_DIM
    o_gla = o_dt + SSD_HEADS
    gla_cols = 2 * GLA_HEADS * GLA_DK + 2 * GLA_WIDTH
    o_lr = o_gla + gla_cols

    h = x.reshape(t, D_MODEL)
    for i in range(depth):
        lambda_init = 0.8 - 0.6 * math.exp(-0.3 * i)
        wi = w_in[i]
        w1 = wi[:, :o_dt].astype(BF16)
        w2 = wi[:, o_gla:o_lr].astype(BF16)
        w3 = jnp.concatenate([wi[:, o_dt:o_gla], wi[:, o_lr:o_lr + GLA_GATE_RANK]], axis=1)
        w3 = _pad_lanes(w3, 0).astype(BF16)
        qg = jnp.tile(da_q_norm[i], 2 * DA_HEADS).reshape(1, qk_w)
        kg = jnp.tile(da_k_norm[i], 2 * DA_HEADS).reshape(1, qk_w)
        (q_da, k_da, v_da, z, xs, bc, small, gq, gk, gv, gg) = _in_proj(
            h, attn_norm[i].reshape(1, D_MODEL), w1, w2, w3, gmat, qg, kg, c_tab, s_tab,
            ssd_conv_w[i], ssd_conv_b[i].reshape(1, -1), seq)

        lam_vecs = jnp.stack([da_lambda_q1[i], da_lambda_k1[i], da_lambda_q2[i], da_lambda_k2[i]])
        lam_consts = jnp.asarray([[lambda_init, 1.0 - lambda_init]], F32)
        lam_row = jnp.concatenate([_pad_lanes(lam_vecs, 0), _pad_lanes(lam_consts, 0),
                                   jnp.zeros((SUBLANES - 5, LANES), F32)], axis=0)
        score_bound = SCORE_NORM * jnp.max(jnp.abs(da_q_norm[i])) * jnp.max(jnp.abs(da_k_norm[i]))
        attn_args = (q_da, k_da, v_da, lam_row, da_sub_norm[i].reshape(1, DA_DV))
        y_da = lax.cond(score_bound <= MAX_UNSHIFTED_SCORE,
                        lambda *a: _diff_attn(*a, batch, seq, False),
                        lambda *a: _diff_attn(*a, batch, seq, True), *attn_args)

        y_ssd = _ssd(z, xs, bc, small,
                     _pad_lanes(ssd_dt_bias[i].reshape(1, -1), DT_LANE0),
                     _pad_lanes(ssd_a_log[i].reshape(1, -1), DT_LANE0),
                     jnp.repeat(ssd_d[i], SSD_HEADDIM).reshape(1, -1), ssd_norm[i].reshape(1, -1),
                     tri_ssd, expand, batch, seq)

        w_g2 = jnp.pad(gla_w_gate2[i], ((LR_LANE0, LANES - LR_LANE0 - GLA_GATE_RANK), (0, 0))).astype(BF16)
        y_gla = _gla(gq, gk, gv, gg, small, w_g2, gla_b_gate[i].reshape(1, -1),
                     jnp.tile(gla_norm[i], GLA_HEADS).reshape(1, -1), tri_gla, batch, seq)

        h = _post(h, y_da, y_ssd, y_gla, p[i].reshape(t, PLE_DIM), w_out[i].astype(BF16),
                  ffn_norm[i].reshape(1, D_MODEL), w_ffn_gate[i].astype(BF16), w_ffn_up[i].astype(BF16),
                  w_ffn_down[i].astype(BF16), ple_w_gate[i].astype(BF16), ple_w_proj[i].astype(BF16))
    return h.reshape(batch, seq, D_MODEL)
```

```python
import functools
import math

import jax
import jax.numpy as jnp
import numpy as np
from jax import lax
from jax.experimental import pallas as pl
from jax.experimental.pallas import tpu as pltpu

D_MODEL = 1024
PLE_DIM = 256
ROPE_THETA = 500000.0
NORM_EPS = 1e-6

DA_HEADS = 4
DA_DQK = 64
DA_DV = 2 * DA_DQK
DA_ROT = DA_DQK // 4
DA_WIDTH = DA_HEADS * DA_DV

SSD_HEADS = 8
SSD_HEADDIM = 64
SSD_DINNER = SSD_HEADS * SSD_HEADDIM
SSD_GROUPS = 2
SSD_DSTATE = 128
SSD_CONV = 4
SSD_CHUNK = 128
SSD_CONV_DIM = SSD_DINNER + 2 * SSD_GROUPS * SSD_DSTATE

GLA_HEADS = 4
GLA_DK = 64
GLA_DV = 128
GLA_GATE_RANK = 16
GLA_GATE_NORMALIZER = 16.0
GLA_CHUNK = 64
GLA_WIDTH = GLA_HEADS * GLA_DV

MIX_WIDTH = DA_WIDTH + SSD_DINNER + GLA_WIDTH
D_FF = -(-8 * D_MODEL // (3 * 256)) * 256

LANES = 128
SUBLANES = 8
VMEM_LIMIT = 56 * 1024 * 1024

DT_LANE0 = 0
LR_LANE0 = SSD_HEADS

TM_PROJ = 512
CONV_COLS = 256
TM_POST = 512
TQ_ATTN = 2048
TK_ATTN = 1024
ATTN_DIAG_STRIP = 256
TS_SSD = 512
TS_GLA = 512
ROPE_BLK = 2048

NEG_BIG = -1e30
LOG2E = math.log2(math.e)
SCORE_NORM = DA_DQK ** 0.5
MAX_UNSHIFTED_SCORE = 60.0
F32 = jnp.float32
BF16 = jnp.bfloat16


def _dot(a, b):
    return jnp.dot(a, b, preferred_element_type=F32)


def _dot_nt(a, b):
    return lax.dot_general(a, b, (((1,), (1,)), ((), ())), preferred_element_type=F32)


def _dot_tn(a, b):
    return lax.dot_general(a, b, (((0,), (0,)), ((), ())), preferred_element_type=F32)


def _split3(x):
    hi = x.astype(BF16)
    r1 = x - hi.astype(F32)
    mid = r1.astype(BF16)
    lo = (r1 - mid.astype(F32)).astype(BF16)
    return hi, mid, lo


def _dot_exact_lhs(m_bf16, x):
    hi, mid, lo = _split3(x)
    return _dot(m_bf16, hi) + _dot(m_bf16, mid) + _dot(m_bf16, lo)


def _dot_exact_rhs(x, e_bf16):
    hi, mid, lo = _split3(x)
    return _dot(hi, e_bf16) + _dot(mid, e_bf16) + _dot(lo, e_bf16)


def _sigmoid(x):
    return 1.0 / (1.0 + jnp.exp(-x))


def _silu(x):
    return x * _sigmoid(x)


def _softplus(x):
    return jnp.maximum(x, 0.0) + jnp.log1p(jnp.exp(-jnp.abs(x)))


def _cparams(n_axes):
    return pltpu.CompilerParams(dimension_semantics=("arbitrary",) * n_axes,
                                vmem_limit_bytes=VMEM_LIMIT)


def _const_spec(shape):
    nd = len(shape)
    return pl.BlockSpec(shape, lambda *_: (0,) * nd, pipeline_mode=pl.Buffered(1))


def _rope_kernel(pos_ref, invf_ref, cos_ref, sin_ref):
    ang = pos_ref[...].astype(F32) * invf_ref[...]
    cos_ref[...] = jnp.cos(ang)
    sin_ref[...] = jnp.sin(ang)


def _rope_tables(positions):
    t = positions.size
    half = DA_ROT // 2
    blk = min(ROPE_BLK, t)
    inv_freq = (ROPE_THETA ** (-jnp.arange(0, DA_ROT, 2, dtype=F32) / DA_ROT)).reshape(half, 1)
    cos8, sin8 = pl.pallas_call(
        _rope_kernel,
        out_shape=(jax.ShapeDtypeStruct((half, t), F32),) * 2,
        grid=(t // blk,),
        in_specs=[pl.BlockSpec((1, blk), lambda i: (0, i)),
                  pl.BlockSpec((half, 1), lambda i: (0, 0))],
        out_specs=(pl.BlockSpec((half, blk), lambda i: (0, i)),) * 2,
        compiler_params=_cparams(1),
        name="rope_tables",
    )(positions.reshape(1, t), inv_freq)
    c = cos8.T
    s = sin8.T
    rest = DA_DQK - DA_ROT
    c64 = jnp.concatenate([c, c, jnp.ones((t, rest), F32)], axis=1)
    s64 = jnp.concatenate([-s, s, jnp.zeros((t, rest), F32)], axis=1)
    reps = LANES // DA_DQK
    return jnp.tile(c64, (1, reps)), jnp.tile(s64, (1, reps))


def _in_proj_kernel(h_ref, gain_ref, w1_ref, w2_ref, w3_ref, gmat_ref, qg_ref, kg_ref, c_ref, s_ref,
                    cw_ref, cb_ref,
                    q_out, k_out, v_out, z_out, xs_out, bc_out, small_out, gq_out, gk_out, gv_out, gg_out,
                    ext_ref, *, tiles_per_seq):
    h = h_ref[...]
    ms = jnp.mean(h * h, axis=-1, keepdims=True)
    a = (h * lax.rsqrt(ms + NORM_EPS) * gain_ref[...]).astype(BF16)
    tm = h.shape[0]
    qk_w = DA_HEADS * 2 * DA_DQK
    n_slab = qk_w // LANES
    half = DA_ROT // 2

    c_tab = c_ref[...]
    s_tab = s_ref[...]
    lane = lax.broadcasted_iota(jnp.int32, (tm, LANES), 1) & (DA_DQK - 1)
    first_half = lane < half

    def norm_rope(raw, g_ref, scale):
        sq = (raw * raw).astype(BF16)
        gw = gmat_ref.shape[0]
        msq = jnp.concatenate([_dot(sq[:, c:c + gw], gmat_ref[...]) for c in range(0, qk_w, gw)],
                              axis=1)
        n = raw * lax.rsqrt(msq + NORM_EPS) * g_ref[...]
        outs = []
        for i in range(n_slab):
            x = n[:, i * LANES:(i + 1) * LANES]
            partner = jnp.where(first_half, pltpu.roll(x, LANES - half, 1), pltpu.roll(x, half, 1))
            outs.append((x * c_tab + partner * s_tab) * scale)
        return jnp.concatenate(outs, axis=1).astype(BF16)

    @pl.when(pl.program_id(0) % tiles_per_seq == 0)
    def _():
        ext_ref[0:SUBLANES, :] = jnp.zeros((SUBLANES, SSD_CONV_DIM), F32)

    o_xbc = 2 * qk_w + DA_WIDTH + SSD_DINNER
    first = SUBLANES - (SSD_CONV - 1)

    def conv_rows(r0, r1):
        conv = cb_ref[...] + cw_ref[0:1, :] * ext_ref[first + r0:first + r1, :]
        for j in range(1, SSD_CONV):
            conv = conv + cw_ref[j:j + 1, :] * ext_ref[first + j + r0:first + j + r1, :]
        act = _silu(conv)
        xs_out[r0:r1, :] = act[:, 0:SSD_DINNER]
        bc_out[r0:r1, :] = act[:, SSD_DINNER:].astype(BF16)

    gqk = GLA_HEADS * GLA_DK
    o_v = 2 * qk_w
    o_z = o_v + DA_WIDTH

    def proj_q():
        q_out[...] = norm_rope(_dot(a, w1_ref[:, 0:qk_w]), qg_ref, DA_DQK ** -0.5 * LOG2E)

    def proj_k():
        k_out[...] = norm_rope(_dot(a, w1_ref[:, qk_w:2 * qk_w]), kg_ref, 1.0)

    def proj_v():
        v_out[...] = _dot(a, w1_ref[:, o_v:o_v + DA_WIDTH]).astype(BF16)

    def proj_z():
        z_out[...] = _dot(a, w1_ref[:, o_z:o_z + SSD_DINNER])

    def proj_small():
        small_out[...] = _dot(a, w3_ref[...])

    def proj_gqk():
        gq_out[...] = _dot(a, w2_ref[:, 0:gqk])
        gk_out[...] = _dot(a, w2_ref[:, gqk:2 * gqk])

    def proj_gv():
        gv_out[...] = _dot(a, w2_ref[:, 2 * gqk:2 * gqk + GLA_WIDTH]).astype(BF16)

    def proj_gg():
        gg_out[...] = _dot(a, w2_ref[:, 2 * gqk + GLA_WIDTH:2 * gqk + 2 * GLA_WIDTH])

    for proj in (proj_q, proj_k, proj_v, proj_z):
        proj()
    ext_ref[SUBLANES:SUBLANES + tm, :] = _dot(a, w1_ref[:, o_xbc:o_xbc + SSD_CONV_DIM])
    conv_rows(0, tm)
    ext_ref[0:SUBLANES, :] = ext_ref[tm:tm + SUBLANES, :]
    for proj in (proj_small, proj_gqk, proj_gv, proj_gg):
        proj()


def _in_proj(h, gain, w1, w2, w3, gmat, qg, kg, c_tab, s_tab, cw, cb, seq):
    t = h.shape[0]
    tm = min(TM_PROJ, seq)
    qk_w = DA_HEADS * 2 * DA_DQK
    gqk = GLA_HEADS * GLA_DK
    widths = [(qk_w, BF16), (qk_w, BF16), (DA_WIDTH, BF16), (SSD_DINNER, F32), (SSD_DINNER, F32),
              (2 * SSD_GROUPS * SSD_DSTATE, BF16), (LANES, F32), (gqk, F32), (gqk, F32),
              (GLA_WIDTH, BF16), (GLA_WIDTH, F32)]
    row = lambda w: pl.BlockSpec((tm, w), lambda i: (i, 0))
    consts = (gain, w1, w2, w3, gmat, qg, kg)
    return pl.pallas_call(
        functools.partial(_in_proj_kernel, tiles_per_seq=seq // tm),
        out_shape=tuple(jax.ShapeDtypeStruct((t, w), d) for w, d in widths),
        grid=(t // tm,),
        in_specs=[row(D_MODEL)] + [_const_spec(a.shape) for a in consts] + [row(LANES), row(LANES)]
        + [_const_spec(cw.shape), _const_spec(cb.shape)],
        out_specs=tuple(row(w) for w, _ in widths),
        scratch_shapes=[pltpu.VMEM((tm + 2 * SUBLANES, SSD_CONV_DIM), F32)],
        compiler_params=_cparams(1),
        name="in_proj",
    )(h, *consts, c_tab, s_tab, cw, cb)


def _attn_kernel(q_ref, k_ref, v_ref, lam_ref, gsub_ref, o_ref, acc1_ref, acc2_ref, *, tq, tk, online):
    qi = pl.program_id(2)
    q = q_ref[0]
    lane = lax.broadcasted_iota(jnp.int32, q.shape, 1)
    zero = jnp.zeros_like(q)
    q1 = jnp.where(lane < DA_DQK, q, zero)
    q2 = jnp.where(lane >= DA_DQK, q, zero)
    ones_col = (lax.broadcasted_iota(jnp.int32, (tk, LANES), 1) == 0).astype(BF16)
    kv_per_q = tq // tk
    n_full = qi * kv_per_q

    def kv_block(j, n=tk):
        start = pl.multiple_of(j * tk, tk)
        kj = k_ref[0, pl.ds(start, n), :]
        ones = ones_col if n == tk else jnp.concatenate([ones_col] * (n // tk), axis=0)
        return kj, jnp.concatenate([v_ref[0, pl.ds(start, n), :], ones], axis=1)

    if online:
        acc1_ref[...] = jnp.zeros_like(acc1_ref)
        acc2_ref[...] = jnp.zeros_like(acc2_ref)

        def block(j, carry, col0):
            kj, vext = kv_block(j)
            new = []
            for qc, m, acc_ref in ((q1, carry[0], acc1_ref), (q2, carry[1], acc2_ref)):
                s = _dot_nt(qc, kj)
                if col0 is not None:
                    row = lax.broadcasted_iota(jnp.int32, s.shape, 0)
                    col = lax.broadcasted_iota(jnp.int32, s.shape, 1) + col0
                    s = jnp.where(col <= row, s, NEG_BIG)
                m_new = jnp.maximum(m, jnp.max(s, axis=-1, keepdims=True))
                alpha = jnp.exp2(m - m_new)
                p = jnp.exp2(s - m_new).astype(BF16)
                acc_ref[...] = alpha * acc_ref[...] + _dot(p, vext)
                new.append(m_new)
            return tuple(new)

        m0 = jnp.full((tq, 1), NEG_BIG, F32)
        carry = lax.fori_loop(0, n_full, lambda j, c: block(j, c, None), (m0, m0))
        for d in range(kv_per_q):
            carry = block(n_full + d, carry, d * tk)
    else:
        def full_block(j, c):
            kj, vext = kv_block(j)
            for qc, acc_ref in ((q1, acc1_ref), (q2, acc2_ref)):
                p = jnp.exp2(_dot_nt(qc, kj)).astype(BF16)
                acc_ref[...] += _dot(p, vext)
            return c

        acc1_ref[...] = jnp.zeros_like(acc1_ref)
        acc2_ref[...] = jnp.zeros_like(acc2_ref)
        lax.fori_loop(0, n_full, full_block, 0)
        kd, vd = kv_block(n_full, tq)
        rs = min(ATTN_DIAG_STRIP, tq)
        for r in range(tq // rs):
            r0, nk = r * rs, (r + 1) * rs
            row = lax.broadcasted_iota(jnp.int32, (rs, nk), 0) + r0
            col = lax.broadcasted_iota(jnp.int32, (rs, nk), 1)
            keep = col <= row
            for qc, acc_ref in ((q1, acc1_ref), (q2, acc2_ref)):
                p = jnp.where(keep, jnp.exp2(_dot_nt(qc[r0:r0 + rs], kd[0:nk])), 0.0).astype(BF16)
                acc_ref[r0:r0 + rs, :] += _dot(p, vd[0:nk])

    a1 = acc1_ref[...]
    a2 = acc2_ref[...]
    o1 = a1[:, :DA_DV] / a1[:, DA_DV:DA_DV + 1]
    o2 = a2[:, :DA_DV] / a2[:, DA_DV:DA_DV + 1]
    lp = lam_ref[...]
    lam = (jnp.exp(jnp.sum(lp[0:1] * lp[1:2], axis=-1, keepdims=True))
           - jnp.exp(jnp.sum(lp[2:3] * lp[3:4], axis=-1, keepdims=True)) + lp[4:5, 0:1])
    post = lp[4:5, 1:2]
    o = o1 - lam * o2
    ms = jnp.mean(o * o, axis=-1, keepdims=True)
    o_ref[0] = (o * lax.rsqrt(ms + NORM_EPS) * gsub_ref[...] * post).astype(o_ref.dtype)


def _diff_attn(q, k, v, lam_row, gsub, batch, seq, online):
    tq = min(TQ_ATTN, seq)
    tk = min(TK_ATTN, tq)
    q3 = q.reshape(batch, seq, DA_WIDTH)
    k3 = k.reshape(batch, seq, DA_WIDTH)
    v3 = v.reshape(batch, seq, DA_WIDTH)
    out = pl.pallas_call(
        functools.partial(_attn_kernel, tq=tq, tk=tk, online=online),
        out_shape=jax.ShapeDtypeStruct((batch, seq, DA_WIDTH), BF16),
        grid=(batch, DA_HEADS, seq // tq),
        in_specs=[pl.BlockSpec((1, tq, DA_DV), lambda b, h, i: (b, i, h)),
                  pl.BlockSpec((1, seq, DA_DV), lambda b, h, i: (b, 0, h)),
                  pl.BlockSpec((1, seq, DA_DV), lambda b, h, i: (b, 0, h)),
                  _const_spec(lam_row.shape), _const_spec(gsub.shape)],
        out_specs=pl.BlockSpec((1, tq, DA_DV), lambda b, h, i: (b, i, h)),
        scratch_shapes=[pltpu.VMEM((tq, 2 * LANES), F32), pltpu.VMEM((tq, 2 * LANES), F32)],
        compiler_params=_cparams(3),
        name="diff_attn_online" if online else "diff_attn",
    )(q3, k3, v3, lam_row, gsub)
    return out.reshape(batch * seq, DA_WIDTH)


def _ssd_kernel(z_ref, xs_ref, bc_ref, small_ref, dtb_ref, alog_ref, dskip_ref, gn_ref,
                tri_ref, exp_ref, y_ref, state_ref, *, ts):
    L = SSD_CHUNK
    N = SSD_DSTATE
    H = SSD_HEADS
    gw = SSD_DINNER // SSD_GROUPS
    rpg = H // SSD_GROUPS
    t = pl.program_id(1)

    @pl.when(t == 0)
    def _():
        state_ref[...] = jnp.zeros_like(state_ref)

    a_neg = -jnp.exp(alog_ref[...])
    tri = tri_ref[...]
    expand = exp_ref[...]
    row_i = lax.broadcasted_iota(jnp.int32, (L, L), 0)
    col_i = lax.broadcasted_iota(jnp.int32, (L, L), 1)
    causal = col_i <= row_i
    head_lane = lax.broadcasted_iota(jnp.int32, (L, LANES), 1) < H
    glane = lax.broadcasted_iota(jnp.int32, (1, gw), 1)
    head_mask = [(glane >= r * SSD_HEADDIM) & (glane < (r + 1) * SSD_HEADDIM) for r in range(rpg)]

    for c in range(ts // L):
        r0 = c * L
        xs = xs_ref[0, r0:r0 + L, :]
        bm = bc_ref[0, r0:r0 + L, 0:SSD_GROUPS * N]
        cm = bc_ref[0, r0:r0 + L, SSD_GROUPS * N:]
        dt = _softplus(small_ref[0, r0:r0 + L, :] + dtb_ref[...])
        cum = _dot_exact_lhs(tri, dt * a_neg)
        cum_t = cum.T
        cum_last = cum[L - 1:L, :]
        packed = (jnp.where(head_lane, dt, 0.0)
                  + pltpu.roll(jnp.where(head_lane, jnp.exp(cum), 0.0), H, 1)
                  + pltpu.roll(jnp.where(head_lane, jnp.exp(cum_last - cum), 0.0), 2 * H, 1))
        hi = packed.astype(BF16)
        lo = (packed - hi.astype(F32)).astype(BF16)
        factors = _dot(hi, expand) + _dot(lo, expand)
        dt_e = factors[:, 0:SSD_DINNER]
        ecum_e = factors[:, SSD_DINNER:2 * SSD_DINNER]
        dte_e = factors[:, 2 * SSD_DINNER:]
        xdt = xs * dt_e
        xdt_b = xdt.astype(BF16)
        xend_b = (xdt * dte_e).astype(BF16)
        chunk_decay = ecum_e[L - 1:L, :]
        y_parts = []
        for g in range(SSD_GROUPS):
            bg = bm[:, g * N:(g + 1) * N]
            cg = cm[:, g * N:(g + 1) * N]
            cb = _dot_nt(cg, bg)
            m_heads = []
            for r in range(rpg):
                hh = g * rpg + r
                seg = cum[:, hh:hh + 1] - cum_t[hh:hh + 1, :]
                decay = jnp.exp(jnp.where(causal, seg, NEG_BIG))
                m_heads.append((cb * decay).astype(BF16))
            xg = xdt_b[:, g * gw:(g + 1) * gw]
            x_bd = jnp.concatenate([jnp.where(head_mask[r], xg, jnp.zeros_like(xg)) for r in range(rpg)],
                                   axis=0)
            y_diag = _dot(jnp.concatenate(m_heads, axis=1), x_bd)
            st = state_ref[g]
            y_off = _dot(cg, st.astype(BF16)) * ecum_e[:, g * gw:(g + 1) * gw]
            new_st = _dot_tn(bg, xend_b[:, g * gw:(g + 1) * gw])
            state_ref[g] = chunk_decay[:, g * gw:(g + 1) * gw] * st + new_st
            y_parts.append(y_diag + y_off)
        y = jnp.concatenate(y_parts, axis=1) + dskip_ref[...] * xs
        y = y * _silu(z_ref[0, r0:r0 + L, :])
        outs = []
        for g in range(SSD_GROUPS):
            yg = y[:, g * gw:(g + 1) * gw]
            ms = jnp.mean(yg * yg, axis=-1, keepdims=True)
            outs.append(yg * lax.rsqrt(ms + NORM_EPS))
        yn = jnp.concatenate(outs, axis=1) * gn_ref[...]
        y_ref[0, r0:r0 + L, :] = yn.astype(y_ref.dtype)


def _ssd(z, xs, bc, small, dtb, alog, dskip, gn, tri, expand, batch, seq):
    ts = min(TS_SSD, seq)
    gw = SSD_DINNER // SSD_GROUPS
    tile = lambda w: pl.BlockSpec((1, ts, w), lambda b, i: (b, i, 0))
    out = pl.pallas_call(
        functools.partial(_ssd_kernel, ts=ts),
        out_shape=jax.ShapeDtypeStruct((batch, seq, SSD_DINNER), BF16),
        grid=(batch, seq // ts),
        in_specs=[tile(SSD_DINNER), tile(SSD_DINNER), tile(2 * SSD_GROUPS * SSD_DSTATE), tile(LANES)]
        + [_const_spec(a.shape) for a in (dtb, alog, dskip, gn, tri, expand)],
        out_specs=tile(SSD_DINNER),
        scratch_shapes=[pltpu.VMEM((SSD_GROUPS, SSD_DSTATE, gw), F32)],
        compiler_params=_cparams(2),
        name="ssd",
    )(z.reshape(batch, seq, -1), xs.reshape(batch, seq, -1), bc.reshape(batch, seq, -1),
      small.reshape(batch, seq, -1), dtb, alog, dskip, gn, tri, expand)
    return out.reshape(batch * seq, SSD_DINNER)


def _gla_kernel(q_ref, k_ref, v_ref, g_ref, small_ref, w2_ref, bg_ref, gn_ref, tri_ref,
                o_ref, state_ref, *, ts):
    L = GLA_CHUNK
    blk = 2 * L
    qk_w = GLA_HEADS * GLA_DK
    t = pl.program_id(1)

    @pl.when(t == 0)
    def _():
        state_ref[...] = jnp.zeros_like(state_ref)

    lane = lax.broadcasted_iota(jnp.int32, (1, qk_w), 1)
    head_mask = [(lane >= h * GLA_DK) & (lane < (h + 1) * GLA_DK) for h in range(GLA_HEADS)]
    row_i = lax.broadcasted_iota(jnp.int32, (GLA_HEADS * blk, blk), 0)
    col_i = lax.broadcasted_iota(jnp.int32, (GLA_HEADS * blk, blk), 1)
    same_chunk = (row_i >= GLA_HEADS * L) == (col_i >= L)
    tril = ((col_i & (L - 1)) <= (row_i & (L - 1))) & same_chunk
    tri = tri_ref[...]

    n_chunk = ts // L
    lr = small_ref[0].astype(BF16)
    gate = _dot(lr, w2_ref[...]) + bg_ref[...]
    gk = (jnp.minimum(gate, 0.0) - jnp.log1p(jnp.exp(-jnp.abs(gate)))) / GLA_GATE_NORMALIZER
    bcum = jnp.concatenate([_dot_exact_lhs(tri, gk[b * blk:(b + 1) * blk]) for b in range(ts // blk)],
                           axis=0)
    k = k_ref[0]
    v = v_ref[0]
    q_dec = q_ref[0] * (GLA_DK ** -0.5) * jnp.exp(bcum)
    k_inv = (k * jnp.exp(-bcum)).astype(BF16)
    lasts = [bcum[c * L + L - 1:c * L + L, :] for c in range(n_chunk)]
    last_rows = jnp.concatenate([jnp.broadcast_to(l, (L, qk_w)) for l in lasts], axis=0)
    k_end = (k * jnp.exp(last_rows - bcum)).astype(BF16)

    news = []
    for c in range(n_chunk):
        full = _dot_tn(v[c * L:(c + 1) * L, :], k_end[c * L:(c + 1) * L, :])
        new = jnp.where(head_mask[0], full[0:GLA_DV, :], 0.0)
        for h in range(1, GLA_HEADS):
            new = new + jnp.where(head_mask[h], full[h * GLA_DV:(h + 1) * GLA_DV, :], 0.0)
        news.append(new)
    st = state_ref[...]
    prev_b = []
    for c in range(n_chunk):
        prev_b.append(st.astype(BF16))
        st = jnp.exp(lasts[c]) * st + news[c]
    state_ref[...] = st

    hl = GLA_HEADS * L
    for b in range(ts // blk):
        r0 = b * blk
        q4 = jnp.concatenate(
            [jnp.where(head_mask[h], q_dec[r0 + ci * L:r0 + (ci + 1) * L], 0.0)
             for ci in range(2) for h in range(GLA_HEADS)], axis=0).astype(BF16)
        att = jnp.where(tril, _dot_nt(q4, k_inv[r0:r0 + blk]), 0.0).astype(BF16)
        inter = [_dot_nt(q4[ci * hl:(ci + 1) * hl], prev_b[2 * b + ci]) for ci in range(2)]
        outs = []
        for h in range(GLA_HEADS):
            att_h = jnp.concatenate([att[ci * hl + h * L:ci * hl + (h + 1) * L] for ci in range(2)], axis=0)
            inter_h = jnp.concatenate([inter[ci][h * L:(h + 1) * L] for ci in range(2)], axis=0)
            o_h = _dot(att_h, v[r0:r0 + blk, h * GLA_DV:(h + 1) * GLA_DV]) + inter_h
            ms = jnp.mean(o_h * o_h, axis=-1, keepdims=True)
            outs.append(o_h * lax.rsqrt(ms + NORM_EPS))
        o = jnp.concatenate(outs, axis=1) * gn_ref[...] * _silu(g_ref[0, r0:r0 + blk, :])
        o_ref[0, r0:r0 + blk, :] = o.astype(o_ref.dtype)


def _gla(q, k, v, g, small, w2, bg, gn, tri, batch, seq):
    ts = min(TS_GLA, seq)
    qk_w = GLA_HEADS * GLA_DK
    tile = lambda w: pl.BlockSpec((1, ts, w), lambda b, i: (b, i, 0))
    out = pl.pallas_call(
        functools.partial(_gla_kernel, ts=ts),
        out_shape=jax.ShapeDtypeStruct((batch, seq, GLA_WIDTH), BF16),
        grid=(batch, seq // ts),
        in_specs=[tile(qk_w), tile(qk_w), tile(GLA_WIDTH), tile(GLA_WIDTH), tile(LANES)]
        + [_const_spec(a.shape) for a in (w2, bg, gn, tri)],
        out_specs=tile(GLA_WIDTH),
        scratch_shapes=[pltpu.VMEM((GLA_DV, qk_w), F32)],
        compiler_params=_cparams(2),
        name="gla",
    )(q.reshape(batch, seq, -1), k.reshape(batch, seq, -1), v.reshape(batch, seq, -1),
      g.reshape(batch, seq, -1), small.reshape(batch, seq, -1), w2, bg, gn, tri)
    return out.reshape(batch * seq, GLA_WIDTH)


FF_CHUNKS = ((0, 1024), (1024, 2048), (2048, D_FF))


def _post_kernel(h_ref, yda_ref, yssd_ref, ygla_ref, p_ref, wout_ref, fgain_ref, wg_ref, wu_ref, wd_ref,
                 wpg_ref, wpp_ref, o_ref):
    mix = jnp.concatenate([yda_ref[...], yssd_ref[...], ygla_ref[...]], axis=1)
    h1 = h_ref[...] + _dot(mix, wout_ref[...])
    ms = jnp.mean(h1 * h1, axis=-1, keepdims=True)
    f = (h1 * lax.rsqrt(ms + NORM_EPS) * fgain_ref[...]).astype(BF16)
    h2 = h1
    for c0, c1 in FF_CHUNKS:
        gate = _dot(f, wg_ref[:, c0:c1])
        up = _dot(f, wu_ref[:, c0:c1])
        h2 = h2 + _dot((_silu(gate) * up).astype(BF16), wd_ref[c0:c1, :])
    emb_gate = _sigmoid(_dot(h2.astype(BF16), wpg_ref[...]))
    emb = _dot(p_ref[...].astype(BF16), wpp_ref[...])
    o_ref[...] = h2 + emb_gate * emb


def _post(h, yda, yssd, ygla, p, wout, fgain, wg, wu, wd, wpg, wpp):
    t = h.shape[0]
    tm = min(TM_POST, t)
    row = lambda w: pl.BlockSpec((tm, w), lambda i: (i, 0))
    return pl.pallas_call(
        _post_kernel,
        out_shape=jax.ShapeDtypeStruct((t, D_MODEL), F32),
        grid=(t // tm,),
        in_specs=[row(D_MODEL), row(DA_WIDTH), row(SSD_DINNER), row(GLA_WIDTH), row(PLE_DIM)]
        + [_const_spec(a.shape) for a in (wout, fgain, wg, wu, wd, wpg, wpp)],
        out_specs=row(D_MODEL),
        compiler_params=_cparams(1),
        name="post",
    )(h, yda, yssd, ygla, p, wout, fgain, wg, wu, wd, wpg, wpp)


def _group_mean_matrix():
    w = 2 * LANES
    g = np.arange(w) // DA_DQK
    return jnp.asarray((g[:, None] == g[None, :]).astype(np.float32) / DA_DQK, BF16)


def _tri_matrix(n, chunk):
    i = np.arange(n)
    m = (i[None, :] <= i[:, None]) & ((i[None, :] // chunk) == (i[:, None] // chunk))
    return jnp.asarray(m.astype(np.float32), BF16)


def _head_expand_matrix():
    n_factors = 3
    m = np.zeros((LANES, n_factors * SSD_DINNER), np.float32)
    for f in range(n_factors):
        for hh in range(SSD_HEADS):
            c0 = f * SSD_DINNER + hh * SSD_HEADDIM
            m[f * SSD_HEADS + hh, c0:c0 + SSD_HEADDIM] = 1.0
    return jnp.asarray(m, BF16)


def _pad_lanes(x, lane0):
    return jnp.pad(x, ((0, 0), (lane0, LANES - lane0 - x.shape[1])))


def kernel(x, p, positions, attn_norm, w_in, da_q_norm, da_k_norm, da_lambda_q1, da_lambda_k1,
           da_lambda_q2, da_lambda_k2, da_sub_norm, ssd_conv_w, ssd_conv_b, ssd_dt_bias, ssd_a_log,
           ssd_d, ssd_norm, gla_w_gate2, gla_b_gate, gla_norm, w_out, ffn_norm, w_ffn_gate, w_ffn_up,
           w_ffn_down, ple_w_proj, ple_w_gate):
    batch, seq, _ = x.shape
    depth = w_in.shape[0]
    t = batch * seq
    c_tab, s_tab = _rope_tables(positions)
    gmat = _group_mean_matrix()
    tri_ssd = _tri_matrix(SSD_CHUNK, SSD_CHUNK)
    tri_gla = _tri_matrix(2 * GLA_CHUNK, GLA_CHUNK)
    expand = _head_expand_matrix()

    qk_w = DA_HEADS * 2 * DA_DQK
    o_dt = 2 * qk_w + DA_WIDTH + SSD_DINNER + SSD_CONV_DIM
    o_gla = o_dt + SSD_HEADS
    gla_cols = 2 * GLA_HEADS * GLA_DK + 2 * GLA_WIDTH
    o_lr = o_gla + gla_cols

    h = x.reshape(t, D_MODEL)
    for i in range(depth):
        lambda_init = 0.8 - 0.6 * math.exp(-0.3 * i)
        wi = w_in[i]
        w1 = wi[:, :o_dt].astype(BF16)
        w2 = wi[:, o_gla:o_lr].astype(BF16)
        w3 = jnp.concatenate([wi[:, o_dt:o_gla], wi[:, o_lr:o_lr + GLA_GATE_RANK]], axis=1)
        w3 = _pad_lanes(w3, 0).astype(BF16)
        qg = jnp.tile(da_q_norm[i], 2 * DA_HEADS).reshape(1, qk_w)
        kg = jnp.tile(da_k_norm[i], 2 * DA_HEADS).reshape(1, qk_w)
        (q_da, k_da, v_da, z, xs, bc, small, gq, gk, gv, gg) = _in_proj(
            h, attn_norm[i].reshape(1, D_MODEL), w1, w2, w3, gmat, qg, kg, c_tab, s_tab,
            ssd_conv_w[i], ssd_conv_b[i].reshape(1, -1), seq)

        lam_vecs = jnp.stack([da_lambda_q1[i], da_lambda_k1[i], da_lambda_q2[i], da_lambda_k2[i]])
        lam_consts = jnp.asarray([[lambda_init, 1.0 - lambda_init]], F32)
        lam_row = jnp.concatenate([_pad_lanes(lam_vecs, 0), _pad_lanes(lam_consts, 0),
                                   jnp.zeros((SUBLANES - 5, LANES), F32)], axis=0)
        score_bound = SCORE_NORM * jnp.max(jnp.abs(da_q_norm[i])) * jnp.max(jnp.abs(da_k_norm[i]))
        attn_args = (q_da, k_da, v_da, lam_row, da_sub_norm[i].reshape(1, DA_DV))
        y_da = lax.cond(score_bound <= MAX_UNSHIFTED_SCORE,
                        lambda *a: _diff_attn(*a, batch, seq, False),
                        lambda *a: _diff_attn(*a, batch, seq, True), *attn_args)

        y_ssd = _ssd(z, xs, bc, small,
                     _pad_lanes(ssd_dt_bias[i].reshape(1, -1), DT_LANE0),
                     _pad_lanes(ssd_a_log[i].reshape(1, -1), DT_LANE0),
                     jnp.repeat(ssd_d[i], SSD_HEADDIM).reshape(1, -1), ssd_norm[i].reshape(1, -1),
                     tri_ssd, expand, batch, seq)

        w_g2 = jnp.pad(gla_w_gate2[i], ((LR_LANE0, LANES - LR_LANE0 - GLA_GATE_RANK), (0, 0))).astype(BF16)
        y_gla = _gla(gq, gk, gv, gg, small, w_g2, gla_b_gate[i].reshape(1, -1),
                     jnp.tile(gla_norm[i], GLA_HEADS).reshape(1, -1), tri_gla, batch, seq)

        h = _post(h, y_da, y_ssd, y_gla, p[i].reshape(t, PLE_DIM), w_out[i].astype(BF16),
                  ffn_norm[i].reshape(1, D_MODEL), w_ffn_gate[i].astype(BF16), w_ffn_up[i].astype(BF16),
                  w_ffn_down[i].astype(BF16), ple_w_gate[i].astype(BF16), ple_w_proj[i].astype(BF16))
    return h.reshape(batch, seq, D_MODEL)
```

```python
import functools
import math

import jax
import jax.numpy as jnp
import numpy as np
from jax import lax
from jax.experimental import pallas as pl
from jax.experimental.pallas import tpu as pltpu

D_MODEL = 1024
PLE_DIM = 256
ROPE_THETA = 500000.0
NORM_EPS = 1e-6

DA_HEADS = 4
DA_DQK = 64
DA_DV = 2 * DA_DQK
DA_ROT = DA_DQK // 4
DA_WIDTH = DA_HEADS * DA_DV

SSD_HEADS = 8
SSD_HEADDIM = 64
SSD_DINNER = SSD_HEADS * SSD_HEADDIM
SSD_GROUPS = 2
SSD_DSTATE = 128
SSD_CONV = 4
SSD_CHUNK = 128
SSD_CONV_DIM = SSD_DINNER + 2 * SSD_GROUPS * SSD_DSTATE

GLA_HEADS = 4
GLA_DK = 64
GLA_DV = 128
GLA_GATE_RANK = 16
GLA_GATE_NORMALIZER = 16.0
GLA_CHUNK = 64
GLA_WIDTH = GLA_HEADS * GLA_DV

MIX_WIDTH = DA_WIDTH + SSD_DINNER + GLA_WIDTH
D_FF = -(-8 * D_MODEL // (3 * 256)) * 256

LANES = 128
SUBLANES = 8
VMEM_LIMIT = 56 * 1024 * 1024

DT_LANE0 = 0
LR_LANE0 = SSD_HEADS

TM_PROJ = 512
CONV_COLS = 256
TM_POST = 512
TQ_ATTN = 2048
TK_ATTN = 1024
ATTN_DIAG_STRIP = 256
TS_SSD = 2048
TS_GLA = 2048
ROPE_BLK = 2048

NEG_BIG = -1e30
LOG2E = math.log2(math.e)
SCORE_NORM = DA_DQK ** 0.5
MAX_UNSHIFTED_SCORE = 60.0
F32 = jnp.float32
BF16 = jnp.bfloat16


def _dot(a, b):
    return jnp.dot(a, b, preferred_element_type=F32)


def _dot_nt(a, b):
    return lax.dot_general(a, b, (((1,), (1,)), ((), ())), preferred_element_type=F32)


def _dot_tn(a, b):
    return lax.dot_general(a, b, (((0,), (0,)), ((), ())), preferred_element_type=F32)


def _split3(x):
    hi = x.astype(BF16)
    r1 = x - hi.astype(F32)
    mid = r1.astype(BF16)
    lo = (r1 - mid.astype(F32)).astype(BF16)
    return hi, mid, lo


def _dot_exact_lhs(m_bf16, x):
    hi, mid, lo = _split3(x)
    return _dot(m_bf16, hi) + _dot(m_bf16, mid) + _dot(m_bf16, lo)


def _dot_exact_rhs(x, e_bf16):
    hi, mid, lo = _split3(x)
    return _dot(hi, e_bf16) + _dot(mid, e_bf16) + _dot(lo, e_bf16)


def _sigmoid(x):
    return 1.0 / (1.0 + jnp.exp(-x))


def _silu(x):
    return x * _sigmoid(x)


def _softplus(x):
    return jnp.maximum(x, 0.0) + jnp.log1p(jnp.exp(-jnp.abs(x)))


def _cparams(n_axes):
    return pltpu.CompilerParams(dimension_semantics=("arbitrary",) * n_axes,
                                vmem_limit_bytes=VMEM_LIMIT)


def _const_spec(shape):
    nd = len(shape)
    return pl.BlockSpec(shape, lambda *_: (0,) * nd, pipeline_mode=pl.Buffered(1))


def _rope_kernel(pos_ref, invf_ref, cos_ref, sin_ref):
    ang = pos_ref[...].astype(F32) * invf_ref[...]
    cos_ref[...] = jnp.cos(ang)
    sin_ref[...] = jnp.sin(ang)


def _rope_tables(positions):
    t = positions.size
    half = DA_ROT // 2
    blk = min(ROPE_BLK, t)
    inv_freq = (ROPE_THETA ** (-jnp.arange(0, DA_ROT, 2, dtype=F32) / DA_ROT)).reshape(half, 1)
    cos8, sin8 = pl.pallas_call(
        _rope_kernel,
        out_shape=(jax.ShapeDtypeStruct((half, t), F32),) * 2,
        grid=(t // blk,),
        in_specs=[pl.BlockSpec((1, blk), lambda i: (0, i)),
                  pl.BlockSpec((half, 1), lambda i: (0, 0))],
        out_specs=(pl.BlockSpec((half, blk), lambda i: (0, i)),) * 2,
        compiler_params=_cparams(1),
        name="rope_tables",
    )(positions.reshape(1, t), inv_freq)
    c = cos8.T
    s = sin8.T
    rest = DA_DQK - DA_ROT
    c64 = jnp.concatenate([c, c, jnp.ones((t, rest), F32)], axis=1)
    s64 = jnp.concatenate([-s, s, jnp.zeros((t, rest), F32)], axis=1)
    reps = LANES // DA_DQK
    return jnp.tile(c64, (1, reps)), jnp.tile(s64, (1, reps))


def _in_proj_kernel(h_ref, gain_ref, w1_ref, w2_ref, w3_ref, gmat_ref, qg_ref, kg_ref, c_ref, s_ref,
                    cw_ref, cb_ref,
                    q_out, k_out, v_out, z_out, xs_out, bc_out, small_out, gq_out, gk_out, gv_out, gg_out,
                    ext_ref, *, tiles_per_seq):
    h = h_ref[...]
    ms = jnp.mean(h * h, axis=-1, keepdims=True)
    a = (h * lax.rsqrt(ms + NORM_EPS) * gain_ref[...]).astype(BF16)
    tm = h.shape[0]
    qk_w = DA_HEADS * 2 * DA_DQK
    n_slab = qk_w // LANES
    half = DA_ROT // 2

    c_tab = c_ref[...]
    s_tab = s_ref[...]
    lane = lax.broadcasted_iota(jnp.int32, (tm, LANES), 1) & (DA_DQK - 1)
    first_half = lane < half

    def norm_rope(raw, g_ref, scale):
        sq = (raw * raw).astype(BF16)
        gw = gmat_ref.shape[0]
        msq = jnp.concatenate([_dot(sq[:, c:c + gw], gmat_ref[...]) for c in range(0, qk_w, gw)],
                              axis=1)
        n = raw * lax.rsqrt(msq + NORM_EPS) * g_ref[...]
        outs = []
        for i in range(n_slab):
            x = n[:, i * LANES:(i + 1) * LANES]
            partner = jnp.where(first_half, pltpu.roll(x, LANES - half, 1), pltpu.roll(x, half, 1))
            outs.append((x * c_tab + partner * s_tab) * scale)
        return jnp.concatenate(outs, axis=1).astype(BF16)

    @pl.when(pl.program_id(0) % tiles_per_seq == 0)
    def _():
        ext_ref[0:SUBLANES, :] = jnp.zeros((SUBLANES, SSD_CONV_DIM), F32)

    o_xbc = 2 * qk_w + DA_WIDTH + SSD_DINNER
    first = SUBLANES - (SSD_CONV - 1)

    def conv_rows(r0, r1):
        conv = cb_ref[...] + cw_ref[0:1, :] * ext_ref[first + r0:first + r1, :]
        for j in range(1, SSD_CONV):
            conv = conv + cw_ref[j:j + 1, :] * ext_ref[first + j + r0:first + j + r1, :]
        act = _silu(conv)
        xs_out[r0:r1, :] = act[:, 0:SSD_DINNER]
        bc_out[r0:r1, :] = act[:, SSD_DINNER:].astype(BF16)

    gqk = GLA_HEADS * GLA_DK
    o_v = 2 * qk_w
    o_z = o_v + DA_WIDTH

    def proj_q():
        q_out[...] = norm_rope(_dot(a, w1_ref[:, 0:qk_w]), qg_ref, DA_DQK ** -0.5 * LOG2E)

    def proj_k():
        k_out[...] = norm_rope(_dot(a, w1_ref[:, qk_w:2 * qk_w]), kg_ref, 1.0)

    def proj_v():
        v_out[...] = _dot(a, w1_ref[:, o_v:o_v + DA_WIDTH]).astype(BF16)

    def proj_z():
        z_out[...] = _dot(a, w1_ref[:, o_z:o_z + SSD_DINNER])

    def proj_small():
        small_out[...] = _dot(a, w3_ref[...])

    def proj_gqk():
        gq_out[...] = _dot(a, w2_ref[:, 0:gqk])
        gk_out[...] = _dot(a, w2_ref[:, gqk:2 * gqk])

    def proj_gv():
        gv_out[...] = _dot(a, w2_ref[:, 2 * gqk:2 * gqk + GLA_WIDTH]).astype(BF16)

    def proj_gg():
        gg_out[...] = _dot(a, w2_ref[:, 2 * gqk + GLA_WIDTH:2 * gqk + 2 * GLA_WIDTH])

    for proj in (proj_q, proj_k, proj_v, proj_z):
        proj()
    ext_ref[SUBLANES:SUBLANES + tm, :] = _dot(a, w1_ref[:, o_xbc:o_xbc + SSD_CONV_DIM])
    conv_rows(0, tm)
    ext_ref[0:SUBLANES, :] = ext_ref[tm:tm + SUBLANES, :]
    for proj in (proj_small, proj_gqk, proj_gv, proj_gg):
        proj()


def _in_proj(h, gain, w1, w2, w3, gmat, qg, kg, c_tab, s_tab, cw, cb, seq):
    t = h.shape[0]
    tm = min(TM_PROJ, seq)
    qk_w = DA_HEADS * 2 * DA_DQK
    gqk = GLA_HEADS * GLA_DK
    widths = [(qk_w, BF16), (qk_w, BF16), (DA_WIDTH, BF16), (SSD_DINNER, F32), (SSD_DINNER, F32),
              (2 * SSD_GROUPS * SSD_DSTATE, BF16), (LANES, F32), (gqk, F32), (gqk, F32),
              (GLA_WIDTH, BF16), (GLA_WIDTH, F32)]
    row = lambda w: pl.BlockSpec((tm, w), lambda i: (i, 0))
    consts = (gain, w1, w2, w3, gmat, qg, kg)
    return pl.pallas_call(
        functools.partial(_in_proj_kernel, tiles_per_seq=seq // tm),
        out_shape=tuple(jax.ShapeDtypeStruct((t, w), d) for w, d in widths),
        grid=(t // tm,),
        in_specs=[row(D_MODEL)] + [_const_spec(a.shape) for a in consts] + [row(LANES), row(LANES)]
        + [_const_spec(cw.shape), _const_spec(cb.shape)],
        out_specs=tuple(row(w) for w, _ in widths),
        scratch_shapes=[pltpu.VMEM((tm + 2 * SUBLANES, SSD_CONV_DIM), F32)],
        compiler_params=_cparams(1),
        name="in_proj",
    )(h, *consts, c_tab, s_tab, cw, cb)


def _attn_kernel(q_ref, k_ref, v_ref, lam_ref, gsub_ref, o_ref, acc1_ref, acc2_ref, *, tq, tk, online):
    qi = pl.program_id(2)
    q = q_ref[0]
    lane = lax.broadcasted_iota(jnp.int32, q.shape, 1)
    zero = jnp.zeros_like(q)
    q1 = jnp.where(lane < DA_DQK, q, zero)
    q2 = jnp.where(lane >= DA_DQK, q, zero)
    ones_col = (lax.broadcasted_iota(jnp.int32, (tk, LANES), 1) == 0).astype(BF16)
    kv_per_q = tq // tk
    n_full = qi * kv_per_q

    def kv_block(j, n=tk):
        start = pl.multiple_of(j * tk, tk)
        kj = k_ref[0, pl.ds(start, n), :]
        ones = ones_col if n == tk else jnp.concatenate([ones_col] * (n // tk), axis=0)
        return kj, jnp.concatenate([v_ref[0, pl.ds(start, n), :], ones], axis=1)

    if online:
        acc1_ref[...] = jnp.zeros_like(acc1_ref)
        acc2_ref[...] = jnp.zeros_like(acc2_ref)

        def block(j, carry, col0):
            kj, vext = kv_block(j)
            new = []
            for qc, m, acc_ref in ((q1, carry[0], acc1_ref), (q2, carry[1], acc2_ref)):
                s = _dot_nt(qc, kj)
                if col0 is not None:
                    row = lax.broadcasted_iota(jnp.int32, s.shape, 0)
                    col = lax.broadcasted_iota(jnp.int32, s.shape, 1) + col0
                    s = jnp.where(col <= row, s, NEG_BIG)
                m_new = jnp.maximum(m, jnp.max(s, axis=-1, keepdims=True))
                alpha = jnp.exp2(m - m_new)
                p = jnp.exp2(s - m_new).astype(BF16)
                acc_ref[...] = alpha * acc_ref[...] + _dot(p, vext)
                new.append(m_new)
            return tuple(new)

        m0 = jnp.full((tq, 1), NEG_BIG, F32)
        carry = lax.fori_loop(0, n_full, lambda j, c: block(j, c, None), (m0, m0))
        for d in range(kv_per_q):
            carry = block(n_full + d, carry, d * tk)
    else:
        def full_block(j, c):
            kj, vext = kv_block(j)
            for qc, acc_ref in ((q1, acc1_ref), (q2, acc2_ref)):
                p = jnp.exp2(_dot_nt(qc, kj)).astype(BF16)
                acc_ref[...] += _dot(p, vext)
            return c

        acc1_ref[...] = jnp.zeros_like(acc1_ref)
        acc2_ref[...] = jnp.zeros_like(acc2_ref)
        lax.fori_loop(0, n_full, full_block, 0)
        kd, vd = kv_block(n_full, tq)
        rs = min(ATTN_DIAG_STRIP, tq)
        for r in range(tq // rs):
            r0, nk = r * rs, (r + 1) * rs
            row = lax.broadcasted_iota(jnp.int32, (rs, nk), 0) + r0
            col = lax.broadcasted_iota(jnp.int32, (rs, nk), 1)
            keep = col <= row
            for qc, acc_ref in ((q1, acc1_ref), (q2, acc2_ref)):
                p = jnp.where(keep, jnp.exp2(_dot_nt(qc[r0:r0 + rs], kd[0:nk])), 0.0).astype(BF16)
                acc_ref[r0:r0 + rs, :] += _dot(p, vd[0:nk])

    a1 = acc1_ref[...]
    a2 = acc2_ref[...]
    o1 = a1[:, :DA_DV] / a1[:, DA_DV:DA_DV + 1]
    o2 = a2[:, :DA_DV] / a2[:, DA_DV:DA_DV + 1]
    lp = lam_ref[...]
    lam = (jnp.exp(jnp.sum(lp[0:1] * lp[1:2], axis=-1, keepdims=True))
           - jnp.exp(jnp.sum(lp[2:3] * lp[3:4], axis=-1, keepdims=True)) + lp[4:5, 0:1])
    post = lp[4:5, 1:2]
    o = o1 - lam * o2
    ms = jnp.mean(o * o, axis=-1, keepdims=True)
    o_ref[0] = (o * lax.rsqrt(ms + NORM_EPS) * gsub_ref[...] * post).astype(o_ref.dtype)


def _diff_attn(q, k, v, lam_row, gsub, batch, seq, online):
    tq = min(TQ_ATTN, seq)
    tk = min(TK_ATTN, tq)
    q3 = q.reshape(batch, seq, DA_WIDTH)
    k3 = k.reshape(batch, seq, DA_WIDTH)
    v3 = v.reshape(batch, seq, DA_WIDTH)
    out = pl.pallas_call(
        functools.partial(_attn_kernel, tq=tq, tk=tk, online=online),
        out_shape=jax.ShapeDtypeStruct((batch, seq, DA_WIDTH), BF16),
        grid=(batch, DA_HEADS, seq // tq),
        in_specs=[pl.BlockSpec((1, tq, DA_DV), lambda b, h, i: (b, i, h)),
                  pl.BlockSpec((1, seq, DA_DV), lambda b, h, i: (b, 0, h)),
                  pl.BlockSpec((1, seq, DA_DV), lambda b, h, i: (b, 0, h)),
                  _const_spec(lam_row.shape), _const_spec(gsub.shape)],
        out_specs=pl.BlockSpec((1, tq, DA_DV), lambda b, h, i: (b, i, h)),
        scratch_shapes=[pltpu.VMEM((tq, 2 * LANES), F32), pltpu.VMEM((tq, 2 * LANES), F32)],
        compiler_params=_cparams(3),
        name="diff_attn_online" if online else "diff_attn",
    )(q3, k3, v3, lam_row, gsub)
    return out.reshape(batch * seq, DA_WIDTH)


def _ssd_kernel(z_ref, xs_ref, bc_ref, small_ref, dtb_ref, alog_ref, dskip_ref, gn_ref,
                tri_ref, exp_ref, y_ref, state_ref, *, ts):
    L = SSD_CHUNK
    N = SSD_DSTATE
    H = SSD_HEADS
    gw = SSD_DINNER // SSD_GROUPS
    rpg = H // SSD_GROUPS
    t = pl.program_id(1)

    @pl.when(t == 0)
    def _():
        state_ref[...] = jnp.zeros_like(state_ref)

    tri = tri_ref[...]
    expand = exp_ref[...]
    row_i = lax.broadcasted_iota(jnp.int32, (L, L), 0)
    col_i = lax.broadcasted_iota(jnp.int32, (L, L), 1)
    causal = col_i <= row_i
    glane = lax.broadcasted_iota(jnp.int32, (1, gw), 1)
    head_mask = [(glane >= r * SSD_HEADDIM) & (glane < (r + 1) * SSD_HEADDIM) for r in range(rpg)]

    n_chunk = ts // L
    small_t = small_ref[0].T
    dt_t = _softplus(small_t[DT_LANE0:DT_LANE0 + H, :] + dtb_ref[...])
    dta_t = dt_t * (-jnp.exp(alog_ref[...]))
    pieces = _split3(dta_t)
    cum_t = jnp.concatenate(
        [sum(_dot_nt(p[:, c * L:(c + 1) * L], tri) for p in pieces) for c in range(n_chunk)], axis=1)
    last_t = jnp.concatenate(
        [jnp.broadcast_to(cum_t[:, c * L + L - 1:(c + 1) * L], (H, L)) for c in range(n_chunk)], axis=1)
    packed_t = jnp.concatenate([dt_t, jnp.exp(cum_t), jnp.exp(last_t - cum_t), jnp.zeros_like(dt_t)], axis=0)
    packed_hi = packed_t.astype(BF16)
    packed_lo = (packed_t - packed_hi.astype(F32)).astype(BF16)
    cum_pad = jnp.concatenate([cum_t, jnp.zeros((LANES - H, ts), F32)], axis=0)

    for c in range(n_chunk):
        r0 = c * L
        xs = xs_ref[0, r0:r0 + L, :]
        bm = bc_ref[0, r0:r0 + L, 0:SSD_GROUPS * N]
        cm = bc_ref[0, r0:r0 + L, SSD_GROUPS * N:]
        cum_t = cum_pad[:, r0:r0 + L]
        cum = cum_t.T
        factors = (_dot_tn(packed_hi[:, r0:r0 + L], expand)
                   + _dot_tn(packed_lo[:, r0:r0 + L], expand))
        dt_e = factors[:, 0:SSD_DINNER]
        ecum_e = factors[:, SSD_DINNER:2 * SSD_DINNER]
        dte_e = factors[:, 2 * SSD_DINNER:]
        xdt = xs * dt_e
        xdt_b = xdt.astype(BF16)
        xend_b = (xdt * dte_e).astype(BF16)
        chunk_decay = ecum_e[L - 1:L, :]
        y_parts = []
        for g in range(SSD_GROUPS):
            bg = bm[:, g * N:(g + 1) * N]
            cg = cm[:, g * N:(g + 1) * N]
            cb = _dot_nt(cg, bg)
            m_heads = []
            for r in range(rpg):
                hh = g * rpg + r
                seg = cum[:, hh:hh + 1] - cum_t[hh:hh + 1, :]
                decay = jnp.exp(jnp.where(causal, seg, NEG_BIG))
                m_heads.append((cb * decay).astype(BF16))
            xg = xdt_b[:, g * gw:(g + 1) * gw]
            x_bd = jnp.concatenate([jnp.where(head_mask[r], xg, jnp.zeros_like(xg)) for r in range(rpg)],
                                   axis=0)
            y_diag = _dot(jnp.concatenate(m_heads, axis=1), x_bd)
            st = state_ref[g]
            y_off = _dot(cg, st.astype(BF16)) * ecum_e[:, g * gw:(g + 1) * gw]
            new_st = _dot_tn(bg, xend_b[:, g * gw:(g + 1) * gw])
            state_ref[g] = chunk_decay[:, g * gw:(g + 1) * gw] * st + new_st
            y_parts.append(y_diag + y_off)
        y = jnp.concatenate(y_parts, axis=1) + dskip_ref[...] * xs
        y = y * _silu(z_ref[0, r0:r0 + L, :])
        outs = []
        for g in range(SSD_GROUPS):
            yg = y[:, g * gw:(g + 1) * gw]
            ms = jnp.mean(yg * yg, axis=-1, keepdims=True)
            outs.append(yg * lax.rsqrt(ms + NORM_EPS))
        yn = jnp.concatenate(outs, axis=1) * gn_ref[...]
        y_ref[0, r0:r0 + L, :] = yn.astype(y_ref.dtype)


def _ssd(z, xs, bc, small, dtb, alog, dskip, gn, tri, expand, batch, seq):
    ts = min(TS_SSD, seq)
    gw = SSD_DINNER // SSD_GROUPS
    tile = lambda w: pl.BlockSpec((1, ts, w), lambda b, i: (b, i, 0))
    out = pl.pallas_call(
        functools.partial(_ssd_kernel, ts=ts),
        out_shape=jax.ShapeDtypeStruct((batch, seq, SSD_DINNER), BF16),
        grid=(batch, seq // ts),
        in_specs=[tile(SSD_DINNER), tile(SSD_DINNER), tile(2 * SSD_GROUPS * SSD_DSTATE), tile(LANES)]
        + [_const_spec(a.shape) for a in (dtb, alog, dskip, gn, tri, expand)],
        out_specs=tile(SSD_DINNER),
        scratch_shapes=[pltpu.VMEM((SSD_GROUPS, SSD_DSTATE, gw), F32)],
        compiler_params=_cparams(2),
        name="ssd",
    )(z.reshape(batch, seq, -1), xs.reshape(batch, seq, -1), bc.reshape(batch, seq, -1),
      small.reshape(batch, seq, -1), dtb, alog, dskip, gn, tri, expand)
    return out.reshape(batch * seq, SSD_DINNER)


def _gla_kernel(q_ref, k_ref, v_ref, g_ref, small_ref, w2_ref, bg_ref, gn_ref, tri_ref,
                o_ref, state_ref, *, ts):
    L = GLA_CHUNK
    blk = 2 * L
    qk_w = GLA_HEADS * GLA_DK
    t = pl.program_id(1)

    @pl.when(t == 0)
    def _():
        state_ref[...] = jnp.zeros_like(state_ref)

    lane = lax.broadcasted_iota(jnp.int32, (1, qk_w), 1)
    head_mask = [(lane >= h * GLA_DK) & (lane < (h + 1) * GLA_DK) for h in range(GLA_HEADS)]
    row_i = lax.broadcasted_iota(jnp.int32, (GLA_HEADS * blk, blk), 0)
    col_i = lax.broadcasted_iota(jnp.int32, (GLA_HEADS * blk, blk), 1)
    same_chunk = (row_i >= GLA_HEADS * L) == (col_i >= L)
    tril = ((col_i & (L - 1)) <= (row_i & (L - 1))) & same_chunk
    tri = tri_ref[...]

    n_chunk = ts // L
    lr = small_ref[0].astype(BF16)
    gate = _dot(lr, w2_ref[...]) + bg_ref[...]
    gk = (jnp.minimum(gate, 0.0) - jnp.log1p(jnp.exp(-jnp.abs(gate)))) / GLA_GATE_NORMALIZER
    bcum = jnp.concatenate([_dot_exact_lhs(tri, gk[b * blk:(b + 1) * blk]) for b in range(ts // blk)],
                           axis=0)
    k = k_ref[0]
    v = v_ref[0]
    q_dec = q_ref[0] * (GLA_DK ** -0.5) * jnp.exp(bcum)
    k_inv = (k * jnp.exp(-bcum)).astype(BF16)
    lasts = [bcum[c * L + L - 1:c * L + L, :] for c in range(n_chunk)]
    last_rows = jnp.concatenate([jnp.broadcast_to(l, (L, qk_w)) for l in lasts], axis=0)
    k_end = (k * jnp.exp(last_rows - bcum)).astype(BF16)

    news = []
    for c in range(n_chunk):
        full = _dot_tn(v[c * L:(c + 1) * L, :], k_end[c * L:(c + 1) * L, :])
        new = jnp.where(head_mask[0], full[0:GLA_DV, :], 0.0)
        for h in range(1, GLA_HEADS):
            new = new + jnp.where(head_mask[h], full[h * GLA_DV:(h + 1) * GLA_DV, :], 0.0)
        news.append(new)
    st = state_ref[...]
    prev_b = []
    for c in range(n_chunk):
        prev_b.append(st.astype(BF16))
        st = jnp.exp(lasts[c]) * st + news[c]
    state_ref[...] = st

    hl = GLA_HEADS * L
    for b in range(ts // blk):
        r0 = b * blk
        q4 = jnp.concatenate(
            [jnp.where(head_mask[h], q_dec[r0 + ci * L:r0 + (ci + 1) * L], 0.0)
             for ci in range(2) for h in range(GLA_HEADS)], axis=0).astype(BF16)
        att = jnp.where(tril, _dot_nt(q4, k_inv[r0:r0 + blk]), 0.0).astype(BF16)
        inter = [_dot_nt(q4[ci * hl:(ci + 1) * hl], prev_b[2 * b + ci]) for ci in range(2)]
        outs = []
        for h in range(GLA_HEADS):
            att_h = jnp.concatenate([att[ci * hl + h * L:ci * hl + (h + 1) * L] for ci in range(2)], axis=0)
            inter_h = jnp.concatenate([inter[ci][h * L:(h + 1) * L] for ci in range(2)], axis=0)
            o_h = _dot(att_h, v[r0:r0 + blk, h * GLA_DV:(h + 1) * GLA_DV]) + inter_h
            ms = jnp.mean(o_h * o_h, axis=-1, keepdims=True)
            outs.append(o_h * lax.rsqrt(ms + NORM_EPS))
        o = jnp.concatenate(outs, axis=1) * gn_ref[...] * _silu(g_ref[0, r0:r0 + blk, :])
        o_ref[0, r0:r0 + blk, :] = o.astype(o_ref.dtype)


def _gla(q, k, v, g, small, w2, bg, gn, tri, batch, seq):
    ts = min(TS_GLA, seq)
    qk_w = GLA_HEADS * GLA_DK
    tile = lambda w: pl.BlockSpec((1, ts, w), lambda b, i: (b, i, 0))
    out = pl.pallas_call(
        functools.partial(_gla_kernel, ts=ts),
        out_shape=jax.ShapeDtypeStruct((batch, seq, GLA_WIDTH), BF16),
        grid=(batch, seq // ts),
        in_specs=[tile(qk_w), tile(qk_w), tile(GLA_WIDTH), tile(GLA_WIDTH), tile(LANES)]
        + [_const_spec(a.shape) for a in (w2, bg, gn, tri)],
        out_specs=tile(GLA_WIDTH),
        scratch_shapes=[pltpu.VMEM((GLA_DV, qk_w), F32)],
        compiler_params=_cparams(2),
        name="gla",
    )(q.reshape(batch, seq, -1), k.reshape(batch, seq, -1), v.reshape(batch, seq, -1),
      g.reshape(batch, seq, -1), small.reshape(batch, seq, -1), w2, bg, gn, tri)
    return out.reshape(batch * seq, GLA_WIDTH)


FF_CHUNKS = ((0, 1024), (1024, 2048), (2048, D_FF))


def _post_kernel(h_ref, yda_ref, yssd_ref, ygla_ref, p_ref, wout_ref, fgain_ref, wg_ref, wu_ref, wd_ref,
                 wpg_ref, wpp_ref, o_ref):
    mix = jnp.concatenate([yda_ref[...], yssd_ref[...], ygla_ref[...]], axis=1)
    h1 = h_ref[...] + _dot(mix, wout_ref[...])
    ms = jnp.mean(h1 * h1, axis=-1, keepdims=True)
    f = (h1 * lax.rsqrt(ms + NORM_EPS) * fgain_ref[...]).astype(BF16)
    h2 = h1
    for c0, c1 in FF_CHUNKS:
        gate = _dot(f, wg_ref[:, c0:c1])
        up = _dot(f, wu_ref[:, c0:c1])
        h2 = h2 + _dot((_silu(gate) * up).astype(BF16), wd_ref[c0:c1, :])
    emb_gate = _sigmoid(_dot(h2.astype(BF16), wpg_ref[...]))
    emb = _dot(p_ref[...].astype(BF16), wpp_ref[...])
    o_ref[...] = h2 + emb_gate * emb


def _post(h, yda, yssd, ygla, p, wout, fgain, wg, wu, wd, wpg, wpp):
    t = h.shape[0]
    tm = min(TM_POST, t)
    row = lambda w: pl.BlockSpec((tm, w), lambda i: (i, 0))
    return pl.pallas_call(
        _post_kernel,
        out_shape=jax.ShapeDtypeStruct((t, D_MODEL), F32),
        grid=(t // tm,),
        in_specs=[row(D_MODEL), row(DA_WIDTH), row(SSD_DINNER), row(GLA_WIDTH), row(PLE_DIM)]
        + [_const_spec(a.shape) for a in (wout, fgain, wg, wu, wd, wpg, wpp)],
        out_specs=row(D_MODEL),
        compiler_params=_cparams(1),
        name="post",
    )(h, yda, yssd, ygla, p, wout, fgain, wg, wu, wd, wpg, wpp)


def _group_mean_matrix():
    w = 2 * LANES
    g = np.arange(w) // DA_DQK
    return jnp.asarray((g[:, None] == g[None, :]).astype(np.float32) / DA_DQK, BF16)


def _tri_matrix(n, chunk):
    i = np.arange(n)
    m = (i[None, :] <= i[:, None]) & ((i[None, :] // chunk) == (i[:, None] // chunk))
    return jnp.asarray(m.astype(np.float32), BF16)


def _head_expand_matrix():
    n_factors = 3
    m = np.zeros(((n_factors + 1) * SSD_HEADS, n_factors * SSD_DINNER), np.float32)
    for f in range(n_factors):
        for hh in range(SSD_HEADS):
            c0 = f * SSD_DINNER + hh * SSD_HEADDIM
            m[f * SSD_HEADS + hh, c0:c0 + SSD_HEADDIM] = 1.0
    return jnp.asarray(m, BF16)


def _pad_lanes(x, lane0):
    return jnp.pad(x, ((0, 0), (lane0, LANES - lane0 - x.shape[1])))


def kernel(x, p, positions, attn_norm, w_in, da_q_norm, da_k_norm, da_lambda_q1, da_lambda_k1,
           da_lambda_q2, da_lambda_k2, da_sub_norm, ssd_conv_w, ssd_conv_b, ssd_dt_bias, ssd_a_log,
           ssd_d, ssd_norm, gla_w_gate2, gla_b_gate, gla_norm, w_out, ffn_norm, w_ffn_gate, w_ffn_up,
           w_ffn_down, ple_w_proj, ple_w_gate):
    batch, seq, _ = x.shape
    depth = w_in.shape[0]
    t = batch * seq
    c_tab, s_tab = _rope_tables(positions)
    gmat = _group_mean_matrix()
    tri_ssd = _tri_matrix(SSD_CHUNK, SSD_CHUNK)
    tri_gla = _tri_matrix(2 * GLA_CHUNK, GLA_CHUNK)
    expand = _head_expand_matrix()

    qk_w = DA_HEADS * 2 * DA_DQK
    o_dt = 2 * qk_w + DA_WIDTH + SSD_DINNER + SSD_CONV_DIM
    o_gla = o_dt + SSD_HEADS
    gla_cols = 2 * GLA_HEADS * GLA_DK + 2 * GLA_WIDTH
    o_lr = o_gla + gla_cols

    h = x.reshape(t, D_MODEL)
    for i in range(depth):
        lambda_init = 0.8 - 0.6 * math.exp(-0.3 * i)
        wi = w_in[i]
        w1 = wi[:, :o_dt].astype(BF16)
        w2 = wi[:, o_gla:o_lr].astype(BF16)
        w3 = jnp.concatenate([wi[:, o_dt:o_gla], wi[:, o_lr:o_lr + GLA_GATE_RANK]], axis=1)
        w3 = _pad_lanes(w3, 0).astype(BF16)
        qg = jnp.tile(da_q_norm[i], 2 * DA_HEADS).reshape(1, qk_w)
        kg = jnp.tile(da_k_norm[i], 2 * DA_HEADS).reshape(1, qk_w)
        (q_da, k_da, v_da, z, xs, bc, small, gq, gk, gv, gg) = _in_proj(
            h, attn_norm[i].reshape(1, D_MODEL), w1, w2, w3, gmat, qg, kg, c_tab, s_tab,
            ssd_conv_w[i], ssd_conv_b[i].reshape(1, -1), seq)

        lam_vecs = jnp.stack([da_lambda_q1[i], da_lambda_k1[i], da_lambda_q2[i], da_lambda_k2[i]])
        lam_consts = jnp.asarray([[lambda_init, 1.0 - lambda_init]], F32)
        lam_row = jnp.concatenate([_pad_lanes(lam_vecs, 0), _pad_lanes(lam_consts, 0),
                                   jnp.zeros((SUBLANES - 5, LANES), F32)], axis=0)
        score_bound = SCORE_NORM * jnp.max(jnp.abs(da_q_norm[i])) * jnp.max(jnp.abs(da_k_norm[i]))
        attn_args = (q_da, k_da, v_da, lam_row, da_sub_norm[i].reshape(1, DA_DV))
        y_da = lax.cond(score_bound <= MAX_UNSHIFTED_SCORE,
                        lambda *a: _diff_attn(*a, batch, seq, False),
                        lambda *a: _diff_attn(*a, batch, seq, True), *attn_args)

        y_ssd = _ssd(z, xs, bc, small,
                     ssd_dt_bias[i].reshape(SSD_HEADS, 1), ssd_a_log[i].reshape(SSD_HEADS, 1),
                     jnp.repeat(ssd_d[i], SSD_HEADDIM).reshape(1, -1), ssd_norm[i].reshape(1, -1),
                     tri_ssd, expand, batch, seq)

        w_g2 = jnp.pad(gla_w_gate2[i], ((LR_LANE0, LANES - LR_LANE0 - GLA_GATE_RANK), (0, 0))).astype(BF16)
        y_gla = _gla(gq, gk, gv, gg, small, w_g2, gla_b_gate[i].reshape(1, -1),
                     jnp.tile(gla_norm[i], GLA_HEADS).reshape(1, -1), tri_gla, batch, seq)

        h = _post(h, y_da, y_ssd, y_gla, p[i].reshape(t, PLE_DIM), w_out[i].astype(BF16),
                  ffn_norm[i].reshape(1, D_MODEL), w_ffn_gate[i].astype(BF16), w_ffn_up[i].astype(BF16),
                  w_ffn_down[i].astype(BF16), ple_w_gate[i].astype(BF16), ple_w_proj[i].astype(BF16))
    return h.reshape(batch, seq, D_MODEL)
```
